```python
import jax, jax.numpy as jnp
from jax import lax
import numpy as np

D_MODEL = 4096
BATCH = 4
SEQ = 2048
DEPTH = 4
DEC_BATCH = 8
DEC_SEQ = 8
PAST_LEN = 8192
PAGE_SIZE = 128

HEAD_DIM = 128
A_GROUPS = ((128, 1), (512, 4), (2048, 16))
N_GROUPS = len(A_GROUPS)
A_HEADS = D_MODEL // HEAD_DIM
A_WIDTH = A_HEADS * HEAD_DIM
A_IN = (3 * N_GROUPS + 1) * A_WIDTH
F_HEADS = D_MODEL // HEAD_DIM
F_WIDTH = F_HEADS * HEAD_DIM
F_IN = 4 * F_WIDTH + F_HEADS
N_MIXERS = 2
N_A_LAYERS = (DEPTH + 1) // 2
N_F_LAYERS = DEPTH // 2
Q_BLOCK = 128
ROPE_THETA = 10000.0
EPS = 1e-6
FORGET_BIAS = 2.0

kernel_name = 'hybrid_dilated_window_fox_step'


def _rmsnorm(x, g):
    xf = x.astype(jnp.float32)
    y = xf * lax.rsqrt(jnp.mean(xf * xf, axis=-1, keepdims=True) + EPS)
    return (y * g.astype(jnp.float32)).astype(x.dtype)


def _rope(x, pos):
    half = HEAD_DIM // 2
    freqs = ROPE_THETA ** (-jnp.arange(half, dtype=jnp.float32) * 2.0 / HEAD_DIM)
    ang = pos.astype(jnp.float32)[:, None] * freqs[None, :]
    cos = jnp.cos(ang)[None, :, None, :]
    sin = jnp.sin(ang)[None, :, None, :]
    xf = x.astype(jnp.float32)
    x1, x2 = xf[..., :half], xf[..., half:]
    return jnp.concatenate([x1 * cos - x2 * sin, x2 * cos + x1 * sin], axis=-1).astype(x.dtype)


def _to_phases(x, d):
    b, s = x.shape[0], x.shape[1]
    x = x.reshape((b, s // d, d) + x.shape[2:])
    x = jnp.swapaxes(x, 1, 2)
    return x.reshape((b * d, s // d) + x.shape[3:])


def _from_phases(x, b, d):
    l = x.shape[1]
    x = x.reshape((b, d, l) + x.shape[2:])
    x = jnp.swapaxes(x, 1, 2)
    return x.reshape((b, l * d) + x.shape[3:])


def _dilated_window_prompt(q, k, v, dil, steps):
    b, s, h, hd = q.shape
    qp, kp, vp = _to_phases(q, dil), _to_phases(k, dil), _to_phases(v, dil)
    n, l = qp.shape[0], qp.shape[1]
    nb = -(-l // steps)
    pad = nb * steps - l

    def blocks(t):
        t = jnp.pad(t, ((0, 0), (0, pad), (0, 0), (0, 0)))
        return t.reshape(n, nb, steps, h, hd)

    def with_prev(t):
        prev = jnp.concatenate([jnp.zeros_like(t[:, :1]), t[:, :-1]], axis=1)
        return jnp.concatenate([prev, t], axis=2)

    qb = blocks(qp)
    kw, vw = with_prev(blocks(kp)), with_prev(blocks(vp))
    scores = jnp.einsum('nbqhd,nbkhd->nbhqk', qb, kw, preferred_element_type=jnp.float32) * (hd ** -0.5)
    qi = jnp.arange(steps)[:, None] + steps
    kj = jnp.arange(2 * steps)[None, :]
    dist = qi - kj
    key_idx = jnp.arange(nb)[:, None, None] * steps - steps + kj[None]
    valid = (dist >= 0)[None] & (dist <= steps)[None] & (key_idx >= 0)
    scores = jnp.where(valid[None, :, None], scores, -jnp.inf)
    m = jnp.max(scores, axis=-1, keepdims=True)
    p = jnp.exp(scores - m)
    den = jnp.sum(p, axis=-1)
    o = jnp.einsum('nbhqk,nbkhd->nbqhd', p, vw.astype(jnp.float32)) / jnp.swapaxes(den, 2, 3)[..., None]
    lse = jnp.swapaxes(m[..., 0] + jnp.log(den), 2, 3)
    o = o.reshape(n, nb * steps, h, hd)[:, :l]
    lse = lse.reshape(n, nb * steps, h)[:, :l]
    return _from_phases(o, b, dil), _from_phases(lse, b, dil)


def _dilated_window_sample(q, k_all, v_all, dil, steps, buf_len):
    n = q.shape[1]
    idx = buf_len + jnp.arange(n)[:, None] - dil * jnp.arange(steps + 1)[None, :]
    valid = idx >= 0
    idx = jnp.maximum(idx, 0)
    kg = k_all[:, idx]
    vg = v_all[:, idx]
    scores = jnp.einsum('bnhd,bnjhd->bnhj', q, kg, preferred_element_type=jnp.float32) * (HEAD_DIM ** -0.5)
    scores = jnp.where(valid[None, :, None, :], scores, -jnp.inf)
    m = jnp.max(scores, axis=-1, keepdims=True)
    p = jnp.exp(scores - m)
    den = jnp.sum(p, axis=-1)
    o = jnp.einsum('bnhj,bnjhd->bnhd', p, vg.astype(jnp.float32)) / den[..., None]
    return o, m[..., 0] + jnp.log(den)


def _merge_groups(outs, lses):
    w = jax.nn.softmax(jnp.stack(lses, 0), axis=0)
    return jnp.einsum('gbsh,gbshd->bshd', w, jnp.stack(outs, 0))


def _gated_out(x, o, gate, w_out):
    b, s = x.shape[0], x.shape[1]
    z = o.reshape(b, s, -1).astype(jnp.float32) * jax.nn.silu(gate.reshape(b, s, -1).astype(jnp.float32))
    return x + jnp.einsum('bse,ed->bsd', z.astype(x.dtype), w_out)


def _a_project(h, w_in, q_gain, k_gain, pos):
    b, s = h.shape[0], h.shape[1]
    parts = jnp.einsum('bsd,de->bse', h, w_in).reshape(b, s, 3 * N_GROUPS + 1, A_HEADS, HEAD_DIM)
    qkv = []
    for g in range(N_GROUPS):
        q = _rope(_rmsnorm(parts[:, :, 3 * g], q_gain[g]), pos)
        k = _rope(_rmsnorm(parts[:, :, 3 * g + 1], k_gain[g]), pos)
        qkv.append((q, k, parts[:, :, 3 * g + 2]))
    return qkv, parts[:, :, 3 * N_GROUPS]


def _mixer_a_prompt(x, norm, w_in, q_gain, k_gain, w_out):
    s = x.shape[1]
    qkv, gate = _a_project(_rmsnorm(x, norm), w_in, q_gain, k_gain, jnp.arange(s))
    outs, lses, ks, vs = [], [], [], []
    for (win, dil), (q, k, v) in zip(A_GROUPS, qkv):
        o, lse = _dilated_window_prompt(q, k, v, dil, win // dil)
        outs.append(o)
        lses.append(lse)
        keep = min(win, s)
        ks.append(k[:, s - keep:])
        vs.append(v[:, s - keep:])
    return _gated_out(x, _merge_groups(outs, lses), gate, w_out), ks, vs


def _mixer_a_sample(x, bufs_k, bufs_v, norm, w_in, q_gain, k_gain, w_out):
    n = x.shape[1]
    qkv, gate = _a_project(_rmsnorm(x, norm), w_in, q_gain, k_gain, PAST_LEN + jnp.arange(n))
    outs, lses, ks, vs = [], [], [], []
    for (win, dil), (q, k, v), bk, bv in zip(A_GROUPS, qkv, bufs_k, bufs_v):
        buf_len = bk.shape[1]
        k_all = jnp.concatenate([bk, k.astype(bk.dtype)], axis=1)
        v_all = jnp.concatenate([bv, v.astype(bv.dtype)], axis=1)
        o, lse = _dilated_window_sample(q, k_all, v_all, dil, win // dil, buf_len)
        outs.append(o)
        lses.append(lse)
        ks.append(k_all[:, n:])
        vs.append(v_all[:, n:])
    return _gated_out(x, _merge_groups(outs, lses), gate, w_out), ks, vs


def _f_project(h, w_in, b_forget, q_gain, k_gain):
    b, s = h.shape[0], h.shape[1]
    proj = jnp.einsum('bsd,de->bse', h, w_in)
    parts = proj[..., :4 * F_WIDTH].reshape(b, s, 4, F_HEADS, HEAD_DIM)
    q = _rmsnorm(parts[:, :, 0], q_gain)
    k = _rmsnorm(parts[:, :, 1], k_gain)
    logf = jax.nn.log_sigmoid(proj[..., 4 * F_WIDTH:].astype(jnp.float32) + b_forget.astype(jnp.float32))
    return q, k, parts[:, :, 2], parts[:, :, 3], logf


def _forgetting_attention_prompt(q, k, v, logf):
    b, s = q.shape[0], q.shape[1]
    nb = s // Q_BLOCK
    c = jnp.swapaxes(jnp.cumsum(logf, axis=1), 1, 2)
    q_blocks = jnp.moveaxis(q.reshape(b, nb, Q_BLOCK, F_HEADS, HEAD_DIM), 1, 0)
    c_blocks = jnp.moveaxis(c.reshape(b, F_HEADS, nb, Q_BLOCK), 2, 0)
    kpos = jnp.arange(s)
    vf = v.astype(jnp.float32)

    def block(args):
        qb, cb, i = args
        scores = jnp.einsum('bqhd,bkhd->bhqk', qb, k, preferred_element_type=jnp.float32) * (HEAD_DIM ** -0.5)
        bias = cb[..., None] - c[:, :, None, :]
        qpos = i * Q_BLOCK + jnp.arange(Q_BLOCK)
        mask = kpos[None, :] <= qpos[:, None]
        p = jax.nn.softmax(jnp.where(mask, scores + bias, -jnp.inf), axis=-1)
        return jnp.einsum('bhqk,bkhd->bqhd', p, vf)

    o = lax.map(block, (q_blocks, c_blocks, jnp.arange(nb)))
    return jnp.moveaxis(o, 0, 1).reshape(b, s, F_HEADS, HEAD_DIM)


def _forgetting_attention_sample(q, k_all, v_all, logf_all, past):
    n = q.shape[1]
    total = k_all.shape[1]
    c = jnp.swapaxes(jnp.cumsum(logf_all, axis=1), 1, 2)
    scores = jnp.einsum('bqhd,bkhd->bhqk', q, k_all, preferred_element_type=jnp.float32) * (HEAD_DIM ** -0.5)
    bias = c[:, :, past:, None] - c[:, :, None, :]
    mask = jnp.arange(total)[None, :] <= (past + jnp.arange(n))[:, None]
    p = jax.nn.softmax(jnp.where(mask, scores + bias, -jnp.inf), axis=-1)
    return jnp.einsum('bhqk,bkhd->bqhd', p, v_all.astype(jnp.float32))


def _mixer_f_prompt(x, norm, w_in, b_forget, q_gain, k_gain, w_out):
    q, k, v, gate, logf = _f_project(_rmsnorm(x, norm), w_in, b_forget, q_gain, k_gain)
    o = _forgetting_attention_prompt(q, k, v, logf)
    return _gated_out(x, o, gate, w_out), k, v, logf


def _mixer_f_sample(x, k_pages, v_pages, logf_pages, page_table, norm, w_in, b_forget, q_gain, k_gain, w_out):
    db = x.shape[0]
    q, k, v, gate, logf = _f_project(_rmsnorm(x, norm), w_in, b_forget, q_gain, k_gain)
    past = page_table.shape[1] * PAGE_SIZE
    k_past = k_pages[page_table].reshape(db, past, F_HEADS, HEAD_DIM)
    v_past = v_pages[page_table].reshape(db, past, F_HEADS, HEAD_DIM)
    lf_past = logf_pages[page_table].reshape(db, past, F_HEADS).astype(jnp.float32)
    k_all = jnp.concatenate([k_past, k.astype(k_past.dtype)], axis=1)
    v_all = jnp.concatenate([v_past, v.astype(v_past.dtype)], axis=1)
    lf_all = jnp.concatenate([lf_past, logf], axis=1)
    o = _forgetting_attention_sample(q, k_all, v_all, lf_all, past)
    return _gated_out(x, o, gate, w_out), k, v, logf


def setup_inputs(seed: int = 0) -> dict:
    key = jax.random.key(seed)
    ks = jax.random.split(key, 24)
    n_pages = PAST_LEN // PAGE_SIZE
    n_pool = (DEC_BATCH * n_pages * 5) // 4

    def nrm(k, shape, scale=1.0):
        return scale * jax.random.normal(k, shape, jnp.float32)

    inputs = {}
    inputs['x_prompt'] = nrm(ks[0], (BATCH, SEQ, D_MODEL))
    inputs['x_sample'] = nrm(ks[1], (DEC_BATCH, DEC_SEQ, D_MODEL))
    for g, (win, dil) in enumerate(A_GROUPS):
        lb = min(win, PAST_LEN)
        inputs['cache_win%d_k' % g] = nrm(ks[2 + 2 * g], (N_A_LAYERS, DEC_BATCH, lb, A_HEADS, HEAD_DIM))
        inputs['cache_win%d_v' % g] = nrm(ks[3 + 2 * g], (N_A_LAYERS, DEC_BATCH, lb, A_HEADS, HEAD_DIM))
    inputs['cache_fox_k'] = nrm(ks[8], (N_F_LAYERS, n_pool, PAGE_SIZE, F_HEADS, HEAD_DIM))
    inputs['cache_fox_v'] = nrm(ks[9], (N_F_LAYERS, n_pool, PAGE_SIZE, F_HEADS, HEAD_DIM))
    inputs['cache_fox_logf'] = jax.nn.log_sigmoid(FORGET_BIAS + nrm(ks[10], (N_F_LAYERS, n_pool, PAGE_SIZE, F_HEADS)))
    perm = jax.random.permutation(ks[11], n_pool)[:DEC_BATCH * n_pages]
    inputs['page_table'] = perm.reshape(DEC_BATCH, n_pages).astype(jnp.int32)
    inputs['a_norm'] = 1.0 + nrm(ks[12], (N_A_LAYERS, D_MODEL), 0.02)
    inputs['a_w_in'] = nrm(ks[13], (N_A_LAYERS, D_MODEL, A_IN), D_MODEL ** -0.5)
    inputs['a_q_gain'] = 1.0 + nrm(ks[14], (N_A_LAYERS, N_GROUPS, HEAD_DIM), 0.02)
    inputs['a_k_gain'] = 1.0 + nrm(ks[15], (N_A_LAYERS, N_GROUPS, HEAD_DIM), 0.02)
    inputs['a_w_out'] = nrm(ks[16], (N_A_LAYERS, A_WIDTH, D_MODEL), A_WIDTH ** -0.5)
    inputs['f_norm'] = 1.0 + nrm(ks[17], (N_F_LAYERS, D_MODEL), 0.02)
    inputs['f_w_in'] = nrm(ks[18], (N_F_LAYERS, D_MODEL, F_IN), D_MODEL ** -0.5)
    inputs['f_b_forget'] = FORGET_BIAS + nrm(ks[19], (N_F_LAYERS, F_HEADS), 0.1)
    inputs['f_q_gain'] = 1.0 + nrm(ks[20], (N_F_LAYERS, HEAD_DIM), 0.02)
    inputs['f_k_gain'] = 1.0 + nrm(ks[21], (N_F_LAYERS, HEAD_DIM), 0.02)
    inputs['f_w_out'] = nrm(ks[22], (N_F_LAYERS, F_WIDTH, D_MODEL), F_WIDTH ** -0.5)
    return inputs


def reference(x_prompt, x_sample, cache_win0_k, cache_win0_v, cache_win1_k, cache_win1_v, cache_win2_k, cache_win2_v,
              cache_fox_k, cache_fox_v, cache_fox_logf, page_table,
              a_norm, a_w_in, a_q_gain, a_k_gain, a_w_out,
              f_norm, f_w_in, f_b_forget, f_q_gain, f_k_gain, f_w_out):
    win_k = (cache_win0_k, cache_win1_k, cache_win2_k)
    win_v = (cache_win0_v, cache_win1_v, cache_win2_v)
    pwk = [[] for _ in range(N_GROUPS)]
    pwv = [[] for _ in range(N_GROUPS)]
    swk = [[] for _ in range(N_GROUPS)]
    swv = [[] for _ in range(N_GROUPS)]
    pfk, pfv, pfl, sfk, sfv, sfl = [], [], [], [], [], []
    yp, ys = x_prompt, x_sample
    for layer in range(DEPTH):
        li = layer // N_MIXERS
        if layer % N_MIXERS == 0:
            yp, kp, vp = _mixer_a_prompt(yp, a_norm[li], a_w_in[li], a_q_gain[li], a_k_gain[li], a_w_out[li])
            ys, ksm, vsm = _mixer_a_sample(ys, [c[li] for c in win_k], [c[li] for c in win_v],
                                           a_norm[li], a_w_in[li], a_q_gain[li], a_k_gain[li], a_w_out[li])
            for g in range(N_GROUPS):
                pwk[g].append(kp[g])
                pwv[g].append(vp[g])
                swk[g].append(ksm[g])
                swv[g].append(vsm[g])
        else:
            yp, kp, vp, lp = _mixer_f_prompt(yp, f_norm[li], f_w_in[li], f_b_forget[li], f_q_gain[li], f_k_gain[li], f_w_out[li])
            ys, ksm, vsm, lsm = _mixer_f_sample(ys, cache_fox_k[li], cache_fox_v[li], cache_fox_logf[li], page_table,
                                                f_norm[li], f_w_in[li], f_b_forget[li], f_q_gain[li], f_k_gain[li], f_w_out[li])
            pfk.append(kp)
            pfv.append(vp)
            pfl.append(lp)
            sfk.append(ksm)
            sfv.append(vsm)
            sfl.append(lsm)
    pw0_k, pw0_v = jnp.stack(pwk[0]), jnp.stack(pwv[0])
    pw1_k, pw1_v = jnp.stack(pwk[1]), jnp.stack(pwv[1])
    pw2_k, pw2_v = jnp.stack(pwk[2]), jnp.stack(pwv[2])
    sw0_k, sw0_v = jnp.stack(swk[0]), jnp.stack(swv[0])
    sw1_k, sw1_v = jnp.stack(swk[1]), jnp.stack(swv[1])
    sw2_k, sw2_v = jnp.stack(swk[2]), jnp.stack(swv[2])
    pf_k, pf_v, pf_logf = jnp.stack(pfk), jnp.stack(pfv), jnp.stack(pfl)
    sf_k, sf_v, sf_logf = jnp.stack(sfk), jnp.stack(sfv), jnp.stack(sfl)
    return (yp, ys, pw0_k, pw0_v, pw1_k, pw1_v, pw2_k, pw2_v, sw0_k, sw0_v, sw1_k, sw1_v, sw2_k, sw2_v,
            pf_k, pf_v, pf_logf, sf_k, sf_v, sf_logf)
```

```python
import functools

import jax
import jax.numpy as jnp
from jax import lax
from jax.experimental import pallas as pl
from jax.experimental.pallas import tpu as pltpu

HEAD_DIM = 128
A_GROUPS = ((128, 1), (512, 4), (2048, 16))
N_GROUPS = len(A_GROUPS)
ROPE_THETA = 10000.0
EPS = 1e-6
PAGES_PER_STEP = 4

VMEM_LIMIT_BYTES = 56 * 1024 * 1024
PROJ_TM = 1024
PROJ_TN = 512
FOX_TQ = 256

F32 = jnp.float32
BF16 = jnp.bfloat16
NEG_INF = float("-inf")


def _cparams(*sem):
    return pltpu.CompilerParams(dimension_semantics=sem, vmem_limit_bytes=VMEM_LIMIT_BYTES)


def _prenorm_kernel(x_ref, g_ref, o_ref):
    x = x_ref[...]
    ms = jnp.mean(x * x, axis=-1, keepdims=True)
    o_ref[...] = (x * lax.rsqrt(ms + EPS) * g_ref[...]).astype(o_ref.dtype)


def _prenorm(x, g):
    m, d = x.shape
    tm = min(m, 512)
    return pl.pallas_call(
        _prenorm_kernel,
        grid=(m // tm,),
        in_specs=[pl.BlockSpec((tm, d), lambda i: (i, 0)), pl.BlockSpec((1, d), lambda i: (0, 0))],
        out_specs=pl.BlockSpec((tm, d), lambda i: (i, 0)),
        out_shape=jax.ShapeDtypeStruct((m, d), BF16),
        compiler_params=_cparams("arbitrary"),
        name="prenorm",
    )(x, g.reshape(1, d))


def _head_norm_rope(acc, gain, cos, sin, out_ref):
    for hh in range(acc.shape[1] // HEAD_DIM):
        sl = slice(hh * HEAD_DIM, (hh + 1) * HEAD_DIM)
        y = acc[:, sl]
        ms = jnp.mean(y * y, axis=-1, keepdims=True)
        y = y * lax.rsqrt(ms + EPS) * gain
        if cos is not None:
            y = y * cos + pltpu.roll(y, HEAD_DIM // 2, 1) * sin
        out_ref[:, sl] = y


def _in_proj_kernel(xp_ref, xs_ref, w_ref, gain_ref, cp_ref, sp_ref, cs_ref, ss_ref,
                    op_ref, os_ref, wbf_ref, *, normed_fn, tiles_per_part, rope):
    j = pl.program_id(0)
    m = pl.program_id(1)

    @pl.when(m == 0)
    def _():
        wbf_ref[...] = w_ref[...].astype(BF16)

    normed = normed_fn(j // tiles_per_part)
    plain = jnp.logical_not(normed)
    gain = gain_ref[...]

    def emit(x_ref, c_ref, s_ref, out_ref):
        acc = jnp.dot(x_ref[...], wbf_ref[...], preferred_element_type=F32)

        @pl.when(normed)
        def _():
            if rope:
                _head_norm_rope(acc, gain, c_ref[...], s_ref[...], out_ref)
            else:
                _head_norm_rope(acc, gain, None, None, out_ref)

        @pl.when(plain)
        def _():
            out_ref[...] = acc

    emit(xp_ref, cp_ref, sp_ref, op_ref)

    @pl.when(m == pl.num_programs(1) - 1)
    def _():
        emit(xs_ref, cs_ref, ss_ref, os_ref)


def _in_proj(hp, hs, w, layer, gains, rope_p, rope_s, *, n_out, normed_fn, rope):
    p_rows, k = hp.shape
    s_rows = hs.shape[0]
    tm = min(PROJ_TM, p_rows)
    tn = min(PROJ_TN, k)
    tiles_per_part = k // tn
    cos_p, sin_p = rope_p
    cos_s, sin_s = rope_s
    pos_tiles = cos_p.shape[0] // tm
    grid = (n_out // tn, p_rows // tm)
    kern = functools.partial(_in_proj_kernel, normed_fn=normed_fn, tiles_per_part=tiles_per_part, rope=rope)
    return pl.pallas_call(
        kern,
        grid=grid,
        in_specs=[
            pl.BlockSpec((tm, k), lambda j, m: (m, 0)),
            pl.BlockSpec((s_rows, k), lambda j, m: (0, 0)),
            pl.BlockSpec((None, k, tn), lambda j, m: (layer, 0, j)),
            pl.BlockSpec((None, 1, HEAD_DIM), lambda j, m: (j // tiles_per_part, 0, 0)),
            pl.BlockSpec((tm, HEAD_DIM), lambda j, m: (m % pos_tiles, 0)),
            pl.BlockSpec((tm, HEAD_DIM), lambda j, m: (m % pos_tiles, 0)),
            pl.BlockSpec((s_rows, HEAD_DIM), lambda j, m: (0, 0)),
            pl.BlockSpec((s_rows, HEAD_DIM), lambda j, m: (0, 0)),
        ],
        out_specs=[
            pl.BlockSpec((tm, tn), lambda j, m: (m, j)),
            pl.BlockSpec((s_rows, tn), lambda j, m: (0, j)),
        ],
        out_shape=[
            jax.ShapeDtypeStruct((p_rows, n_out), F32),
            jax.ShapeDtypeStruct((s_rows, n_out), F32),
        ],
        scratch_shapes=[pltpu.VMEM((k, tn), BF16)],
        compiler_params=_cparams("arbitrary", "arbitrary"),
        name="in_proj",
    )(hp, hs, w, gains, cos_p, sin_p, cos_s, sin_s)


def _out_proj_kernel(zp_ref, zs_ref, w_ref, rp_ref, rs_ref, op_ref, os_ref, wbf_ref):
    m = pl.program_id(1)

    @pl.when(m == 0)
    def _():
        wbf_ref[...] = w_ref[...].astype(BF16)

    op_ref[...] = rp_ref[...] + jnp.dot(zp_ref[...], wbf_ref[...], preferred_element_type=F32)

    @pl.when(m == pl.num_programs(1) - 1)
    def _():
        os_ref[...] = rs_ref[...] + jnp.dot(zs_ref[...].astype(BF16), wbf_ref[...],
                                            preferred_element_type=F32)


def _out_proj(zp, zs, w, layer, xp, xs):
    p_rows, k = zp.shape
    s_rows = zs.shape[0]
    n = w.shape[2]
    tm = min(PROJ_TM, p_rows)
    tn = min(PROJ_TN, n)
    return pl.pallas_call(
        _out_proj_kernel,
        grid=(n // tn, p_rows // tm),
        in_specs=[
            pl.BlockSpec((tm, k), lambda j, m: (m, 0)),
            pl.BlockSpec((s_rows, k), lambda j, m: (0, 0)),
            pl.BlockSpec((None, k, tn), lambda j, m: (layer, 0, j)),
            pl.BlockSpec((tm, tn), lambda j, m: (m, j)),
            pl.BlockSpec((s_rows, tn), lambda j, m: (0, j)),
        ],
        out_specs=[
            pl.BlockSpec((tm, tn), lambda j, m: (m, j)),
            pl.BlockSpec((s_rows, tn), lambda j, m: (0, j)),
        ],
        out_shape=[
            jax.ShapeDtypeStruct((p_rows, n), F32),
            jax.ShapeDtypeStruct((s_rows, n), F32),
        ],
        scratch_shapes=[pltpu.VMEM((k, tn), BF16)],
        compiler_params=_cparams("arbitrary", "arbitrary"),
        name="out_proj",
    )(zp, zs, w, xp, xs)


def _silu(x):
    return x * (1.0 / (1.0 + jnp.exp(-x)))


def _qk(q, k):
    return lax.dot_general(q.astype(BF16), k.astype(BF16), (((1,), (1,)), ((), ())),
                           preferred_element_type=F32)


def _pv(p, v):
    return jnp.dot(p.astype(BF16), v.astype(BF16), preferred_element_type=F32)


def _softmax_pv(s, v):
    mx = jnp.max(s, axis=-1, keepdims=True)
    p = jnp.exp(s - mx)
    den = jnp.sum(p, axis=-1, keepdims=True)
    o = _pv(p, v) * (1.0 / den)
    return o, mx + jnp.log(den)


def _merge_groups(os_, lses):
    mx = functools.reduce(jnp.maximum, lses)
    ws = [jnp.exp(l - mx) for l in lses]
    tot = functools.reduce(lambda a, b: a + b, ws)
    acc = functools.reduce(lambda a, b: a + b, [w * o for w, o in zip(ws, os_)])
    return acc * (1.0 / tot)


def _attn_a_prompt_kernel(*refs, seq):
    qkv = refs[:3 * N_GROUPS]
    gate_ref = refs[3 * N_GROUPS]
    z_ref = refs[3 * N_GROUPS + 1]
    o_scr, lse_scr = refs[3 * N_GROUPS + 2:]
    scale = HEAD_DIM ** -0.5

    for g, (win, dil) in enumerate(A_GROUPS):
        q_ref, k_ref, v_ref = qkv[3 * g:3 * g + 3]
        steps = win // dil
        n_blocks = (seq // dil) // steps
        r = lax.broadcasted_iota(jnp.int32, (steps, steps), 0)
        c = lax.broadcasted_iota(jnp.int32, (steps, steps), 1)
        first_mask = c <= r
        r2 = lax.broadcasted_iota(jnp.int32, (steps, 2 * steps), 0)
        c2 = lax.broadcasted_iota(jnp.int32, (steps, 2 * steps), 1)
        rest_mask = (c2 >= r2) & (c2 <= r2 + steps)

        def rows(start, n):
            return pl.ds(start, n, stride=dil) if dil > 1 else pl.ds(start, n)

        for phase in range(dil):
            for blk in range(n_blocks):
                q_rows = rows(phase + dil * steps * blk, steps)
                if blk == 0:
                    k_rows, mask = q_rows, first_mask
                else:
                    k_rows, mask = rows(phase + dil * steps * (blk - 1), 2 * steps), rest_mask
                s = _qk(q_ref[q_rows, :], k_ref[k_rows, :]) * scale
                s = jnp.where(mask, s, NEG_INF)
                o, lse = _softmax_pv(s, v_ref[k_rows, :])
                o_scr[g, q_rows, :] = o
                lse_scr[g, q_rows, :] = jnp.broadcast_to(lse, (steps, HEAD_DIM))

    merged = _merge_groups([o_scr[g] for g in range(N_GROUPS)], [lse_scr[g] for g in range(N_GROUPS)])
    z_ref[...] = (merged * _silu(gate_ref[...])).astype(z_ref.dtype)


def _attn_a_prompt(parts, *, batch, seq, heads):
    def col(part):
        return pl.BlockSpec((seq, HEAD_DIM), lambda b, h: (b, part * heads + h))

    n_in = 3 * N_GROUPS + 1
    return pl.pallas_call(
        functools.partial(_attn_a_prompt_kernel, seq=seq),
        grid=(batch, heads),
        in_specs=[col(p) for p in range(n_in)],
        out_specs=pl.BlockSpec((seq, HEAD_DIM), lambda b, h: (b, h)),
        out_shape=jax.ShapeDtypeStruct((batch * seq, heads * HEAD_DIM), BF16),
        scratch_shapes=[pltpu.VMEM((N_GROUPS, seq, HEAD_DIM), F32),
                        pltpu.VMEM((N_GROUPS, seq, HEAD_DIM), F32)],
        compiler_params=_cparams("arbitrary", "arbitrary"),
        name="attn_a_prompt",
    )(*([parts] * n_in))


def _attn_a_sample_kernel(*refs, n_new):
    new = refs[:3 * N_GROUPS]
    gate_ref = refs[3 * N_GROUPS]
    caches = refs[3 * N_GROUPS + 1:5 * N_GROUPS + 1]
    z_ref = refs[5 * N_GROUPS + 1]
    scale = HEAD_DIM ** -0.5
    pad = jnp.zeros((HEAD_DIM - n_new, HEAD_DIM), F32)

    for hh in range(gate_ref.shape[1] // HEAD_DIM):
        cols = slice(hh * HEAD_DIM, (hh + 1) * HEAD_DIM)
        outs, lses = [], []
        for g, (win, dil) in enumerate(A_GROUPS):
            q_ref, kn_ref, vn_ref = new[3 * g:3 * g + 3]
            kc_ref, vc_ref = caches[2 * g:2 * g + 2]
            buf = kc_ref.shape[0]
            k_all = jnp.concatenate([kc_ref[:, hh, :], kn_ref[:, cols], pad], axis=0)
            v_all = jnp.concatenate([vc_ref[:, hh, :], vn_ref[:, cols], pad], axis=0)
            s = _qk(q_ref[:, cols], k_all) * scale
            i = lax.broadcasted_iota(jnp.int32, s.shape, 0)
            pos = lax.broadcasted_iota(jnp.int32, s.shape, 1)
            back = buf + i - pos
            valid = (back >= 0) & (back <= win) & ((back & (dil - 1)) == 0)
            s = jnp.where(valid, s, NEG_INF)
            o, lse = _softmax_pv(s, v_all)
            outs.append(o)
            lses.append(lse)
        z_ref[:, cols] = _merge_groups(outs, lses) * _silu(gate_ref[:, cols])


def _attn_a_sample(parts, caches, layer, *, batch, n_new, heads):
    hg = min(8, heads)
    n_hg = heads // hg

    def col(part):
        return pl.BlockSpec((n_new, hg * HEAD_DIM), lambda b, h: (b, part * n_hg + h))

    def cache_spec(c):
        return pl.BlockSpec((None, None, c.shape[2], hg, HEAD_DIM), lambda b, h: (layer, b, 0, h, 0))

    n_in = 3 * N_GROUPS + 1
    flat = [c for kv in caches for c in kv]
    return pl.pallas_call(
        functools.partial(_attn_a_sample_kernel, n_new=n_new),
        grid=(batch, n_hg),
        in_specs=[col(p) for p in range(n_in)] + [cache_spec(c) for c in flat],
        out_specs=pl.BlockSpec((n_new, hg * HEAD_DIM), lambda b, h: (b, h)),
        out_shape=jax.ShapeDtypeStruct((batch * n_new, heads * HEAD_DIM), F32),
        compiler_params=_cparams("arbitrary", "arbitrary"),
        name="attn_a_sample",
    )(*([parts] * n_in), *flat)


def _log_sigmoid(x):
    return jnp.minimum(x, 0.0) - jnp.log(1.0 + jnp.exp(-jnp.abs(x)))


def _cumsum_lanes(x):
    n = x.shape[-1]
    lane = lax.broadcasted_iota(jnp.int32, x.shape, x.ndim - 1)
    shift = 1
    while shift < n:
        x = x + jnp.where(lane >= shift, pltpu.roll(x, shift, x.ndim - 1), 0.0)
        shift *= 2
    return x


def _forget_kernel(h_ref, w_ref, wt_ref, b_ref, bt_ref, lf_ref, lft_ref, ct_ref):
    h = h_ref[...]
    lf_ref[...] = _log_sigmoid(jnp.dot(h, w_ref[...].astype(BF16), preferred_element_type=F32) + b_ref[...])
    lft = _log_sigmoid(_qk(wt_ref[...], h) + bt_ref[...])
    lft_ref[...] = lft
    ct_ref[...] = _cumsum_lanes(lft)


def _forget_rows_kernel(h_ref, w_ref, b_ref, lf_ref):
    lf_ref[...] = _log_sigmoid(
        jnp.dot(h_ref[...], w_ref[...].astype(BF16), preferred_element_type=F32) + b_ref[...])


def _forget_rows(h, w_f, b_f):
    m, d = h.shape
    nh = w_f.shape[1]
    return pl.pallas_call(
        _forget_rows_kernel,
        grid=(1,),
        in_specs=[pl.BlockSpec((m, d), lambda i: (0, 0)),
                  pl.BlockSpec((d, nh), lambda i: (0, 0)),
                  pl.BlockSpec((1, nh), lambda i: (0, 0))],
        out_specs=pl.BlockSpec((m, nh), lambda i: (0, 0)),
        out_shape=jax.ShapeDtypeStruct((m, nh), F32),
        compiler_params=_cparams("arbitrary"),
        name="forget_rows",
    )(h, w_f, b_f.reshape(1, nh))


def _forget(h, w_f, b_f, *, batch):
    m, d = h.shape
    rows = m // batch
    nh = w_f.shape[1]
    t_shape = jax.ShapeDtypeStruct((batch, nh, rows), F32)
    t_spec = pl.BlockSpec((None, nh, rows), lambda b: (b, 0, 0))
    return pl.pallas_call(
        _forget_kernel,
        grid=(batch,),
        in_specs=[
            pl.BlockSpec((rows, d), lambda b: (b, 0)),
            pl.BlockSpec((d, nh), lambda b: (0, 0)),
            pl.BlockSpec((nh, d), lambda b: (0, 0)),
            pl.BlockSpec((1, nh), lambda b: (0, 0)),
            pl.BlockSpec((nh, 1), lambda b: (0, 0)),
        ],
        out_specs=[pl.BlockSpec((rows, nh), lambda b: (b, 0)), t_spec, t_spec],
        out_shape=[jax.ShapeDtypeStruct((m, nh), F32), t_shape, t_shape],
        compiler_params=_cparams("arbitrary"),
        name="forget",
    )(h, w_f, w_f.T, b_f.reshape(1, nh), b_f.reshape(nh, 1))


def _attn_f_prompt_kernel(q_ref, k_ref, v_ref, gate_ref, c_ref, z_ref, *, seq, tq):
    scale = HEAD_DIM ** -0.5
    hh = pl.program_id(1) % c_ref.shape[0]
    c_row = c_ref[pl.ds(hh, 1), :]
    r = lax.broadcasted_iota(jnp.int32, (tq, tq), 0)
    c = lax.broadcasted_iota(jnp.int32, (tq, tq), 1)
    diag_mask = c <= r

    for qi in range(seq // tq):
        q_rows = slice(qi * tq, (qi + 1) * tq)
        q = q_ref[q_rows, :].astype(BF16)
        c_q = jnp.broadcast_to(c_row[:, q_rows], (tq, tq)).T
        m_i = jnp.full((tq, 1), NEG_INF, F32)
        l_i = jnp.zeros((tq, 1), F32)
        acc = jnp.zeros((tq, HEAD_DIM), F32)
        for kj in range(qi + 1):
            k_rows = slice(kj * tq, (kj + 1) * tq)
            s = _qk(q, k_ref[k_rows, :]) * scale + (c_q - c_row[:, k_rows])
            if kj == qi:
                s = jnp.where(diag_mask, s, NEG_INF)
            m_new = jnp.maximum(m_i, jnp.max(s, axis=-1, keepdims=True))
            alpha = jnp.exp(m_i - m_new)
            p = jnp.exp(s - m_new)
            l_i = alpha * l_i + jnp.sum(p, axis=-1, keepdims=True)
            acc = alpha * acc + _pv(p, v_ref[k_rows, :])
            m_i = m_new
        o = acc * (1.0 / l_i)
        z_ref[q_rows, :] = (o * _silu(gate_ref[q_rows, :])).astype(z_ref.dtype)


def _attn_f_prompt(parts, c_t, *, batch, seq, heads):
    def col(part):
        return pl.BlockSpec((seq, HEAD_DIM), lambda b, h: (b, part * heads + h))

    tq = min(FOX_TQ, seq)
    c_rows = min(8, heads)
    return pl.pallas_call(
        functools.partial(_attn_f_prompt_kernel, seq=seq, tq=tq),
        grid=(batch, heads),
        in_specs=[col(0), col(1), col(2), col(3),
                  pl.BlockSpec((None, c_rows, seq), lambda b, h: (b, h // c_rows, 0))],
        out_specs=pl.BlockSpec((seq, HEAD_DIM), lambda b, h: (b, h)),
        out_shape=jax.ShapeDtypeStruct((batch * seq, heads * HEAD_DIM), BF16),
        compiler_params=_cparams("arbitrary", "arbitrary"),
        name="attn_f_prompt",
    )(parts, parts, parts, parts, c_t)


def _page_prefix_kernel(pt_ref, lft_ref, c_ref, tot_ref, carry_ref):
    j = pl.program_id(1)

    @pl.when(j == 0)
    def _():
        carry_ref[...] = jnp.zeros_like(carry_ref)

    c = _cumsum_lanes(lft_ref[...]) + carry_ref[...]
    c_ref[...] = c
    last = jnp.broadcast_to(c[:, -1:], c.shape)
    carry_ref[...] = last
    tot_ref[...] = last


def _page_prefix(lft_pages, page_table, layer):
    _, _, nh, page = lft_pages.shape
    batch, n_pages = page_table.shape
    return pl.pallas_call(
        _page_prefix_kernel,
        grid_spec=pltpu.PrefetchScalarGridSpec(
            num_scalar_prefetch=1,
            grid=(batch, n_pages),
            in_specs=[pl.BlockSpec((None, None, nh, page), lambda b, j, pt: (layer, pt[b, j], 0, 0))],
            out_specs=[pl.BlockSpec((None, nh, page), lambda b, j, pt: (b, 0, j)),
                       pl.BlockSpec((None, nh, page), lambda b, j, pt: (b, 0, 0))],
            scratch_shapes=[pltpu.VMEM((nh, page), F32)],
        ),
        out_shape=[jax.ShapeDtypeStruct((batch, nh, n_pages * page), F32),
                   jax.ShapeDtypeStruct((batch, nh, page), F32)],
        compiler_params=_cparams("arbitrary", "arbitrary"),
        name="page_prefix",
    )(page_table, lft_pages)


def _attn_f_sample_kernel(*refs, heads, n_new, page, pages_per_step):
    g = pages_per_step
    pt_ref = refs[0]
    del pt_ref
    q_ref, kn_ref, vn_ref, gate_ref, lfn_ref, c_ref, tot_ref = refs[1:8]
    k_refs = refs[8:8 + g]
    v_refs = refs[8 + g:8 + 2 * g]
    z_ref = refs[8 + 2 * g]
    m_scr, l_scr, acc_scr = refs[9 + 2 * g:]
    j = pl.program_id(1)
    scale = HEAD_DIM ** -0.5

    @pl.when(j == 0)
    def _():
        m_scr[...] = jnp.full_like(m_scr, NEG_INF)
        l_scr[...] = jnp.zeros_like(l_scr)
        acc_scr[...] = jnp.zeros_like(acc_scr)

    lfn = lfn_ref[...]
    row = lax.broadcasted_iota(jnp.int32, lfn.shape, 0)
    shift = 1
    while shift < n_new:
        lfn = lfn + jnp.where(row >= shift, pltpu.roll(lfn, shift, 0), 0.0)
        shift *= 2
    tot = tot_ref[...]
    c_past = c_ref[...]

    def update(h, s, v):
        rows = slice(h * n_new, (h + 1) * n_new)
        m_old = m_scr[rows, :]
        m_new = jnp.maximum(m_old, jnp.max(s, axis=-1, keepdims=True))
        alpha = jnp.exp(m_old - m_new)
        p = jnp.exp(s - m_new)
        l_scr[rows, :] = alpha * l_scr[rows, :] + jnp.sum(p, axis=-1, keepdims=True)
        acc_scr[rows, :] = alpha * acc_scr[rows, :] + _pv(p, v)
        m_scr[rows, :] = m_new

    for h in range(heads):
        cols = slice(h * HEAD_DIM, (h + 1) * HEAD_DIM)
        q = q_ref[:, cols]
        c_t = lfn[:, h:h + 1] + tot[h:h + 1, 0:1]
        for i in range(g):
            s = _qk(q, k_refs[i][:, h, :]) * scale + (c_t - c_past[h:h + 1, i * page:(i + 1) * page])
            update(h, s, v_refs[i][:, h, :])

    @pl.when(j == pl.num_programs(1) - 1)
    def _():
        r = lax.broadcasted_iota(jnp.int32, (n_new, HEAD_DIM), 0)
        c = lax.broadcasted_iota(jnp.int32, (n_new, HEAD_DIM), 1)
        causal = c <= r
        pad = jnp.zeros((HEAD_DIM - n_new, HEAD_DIM), F32)
        for h in range(heads):
            cols = slice(h * HEAD_DIM, (h + 1) * HEAD_DIM)
            rows = slice(h * n_new, (h + 1) * n_new)
            kn = jnp.concatenate([kn_ref[:, cols], pad], axis=0)
            vn = jnp.concatenate([vn_ref[:, cols], pad], axis=0)
            c_new = lfn[:, h:h + 1]
            c_new_row = jnp.sum(jnp.where(r == c, c_new, 0.0), axis=0, keepdims=True)
            s = _qk(q_ref[:, cols], kn) * scale + (c_new - c_new_row)
            s = jnp.where(causal, s, NEG_INF)
            update(h, s, vn)
            o = acc_scr[rows, :] * (1.0 / l_scr[rows, :])
            z_ref[:, cols] = o * _silu(gate_ref[:, cols])


def _attn_f_sample(parts, lf_new, c_past, tot, k_pages, v_pages, page_table, layer, *, batch, n_new, heads):
    page = k_pages.shape[2]
    n_pages = page_table.shape[1]
    g = PAGES_PER_STEP if n_pages % PAGES_PER_STEP == 0 else 1
    width = heads * HEAD_DIM

    def col(part):
        return pl.BlockSpec((n_new, width), lambda b, j, pt: (b, part))

    def page_spec(i):
        return pl.BlockSpec((None, None, page, heads, HEAD_DIM),
                            lambda b, j, pt: (layer, pt[b, j * g + i], 0, 0, 0))

    kern = functools.partial(_attn_f_sample_kernel, heads=heads, n_new=n_new, page=page, pages_per_step=g)
    return pl.pallas_call(
        kern,
        grid_spec=pltpu.PrefetchScalarGridSpec(
            num_scalar_prefetch=1,
            grid=(batch, n_pages // g),
            in_specs=[col(0), col(1), col(2), col(3),
                      pl.BlockSpec((n_new, heads), lambda b, j, pt: (b, 0)),
                      pl.BlockSpec((None, heads, g * page), lambda b, j, pt: (b, 0, j)),
                      pl.BlockSpec((None, heads, page), lambda b, j, pt: (b, 0, 0))]
            + [page_spec(i) for i in range(g)] + [page_spec(i) for i in range(g)],
            out_specs=pl.BlockSpec((n_new, width), lambda b, j, pt: (b, 0)),
            scratch_shapes=[pltpu.VMEM((heads * n_new, 1), F32),
                            pltpu.VMEM((heads * n_new, 1), F32),
                            pltpu.VMEM((heads * n_new, HEAD_DIM), F32)],
        ),
        out_shape=jax.ShapeDtypeStruct((batch * n_new, width), F32),
        compiler_params=_cparams("arbitrary", "arbitrary"),
        name="attn_f_sample",
    )(page_table, parts, parts, parts, parts, lf_new, c_past, tot,
      *([k_pages] * g), *([v_pages] * g))


def _rope_tables(pos):
    half = HEAD_DIM // 2
    freqs = ROPE_THETA ** (-jnp.arange(half, dtype=F32) * 2.0 / HEAD_DIM)
    ang = pos.astype(F32)[:, None] * freqs[None, :]
    cos, sin = jnp.cos(ang), jnp.sin(ang)
    return jnp.concatenate([cos, cos], axis=-1), jnp.concatenate([-sin, sin], axis=-1)


def _a_normed(part):
    return (part % 3 != 2) & (part != 3 * N_GROUPS)


def _f_normed(part):
    return part < 2


def kernel(x_prompt, x_sample, cache_win0_k, cache_win0_v, cache_win1_k, cache_win1_v, cache_win2_k, cache_win2_v,
           cache_fox_k, cache_fox_v, cache_fox_logf, page_table,
           a_norm, a_w_in, a_q_gain, a_k_gain, a_w_out,
           f_norm, f_w_in, f_b_forget, f_q_gain, f_k_gain, f_w_out):
    batch, seq, d = x_prompt.shape
    dec_batch, n_new, _ = x_sample.shape
    heads = d // HEAD_DIM
    depth = a_norm.shape[0] + f_norm.shape[0]
    past = page_table.shape[1] * cache_fox_k.shape[2]

    win_caches = [(cache_win0_k, cache_win0_v), (cache_win1_k, cache_win1_v), (cache_win2_k, cache_win2_v)]
    fox_lft = jnp.swapaxes(cache_fox_logf, 2, 3)

    rope_p = _rope_tables(jnp.arange(seq))
    rope_s = _rope_tables(past + jnp.tile(jnp.arange(n_new), dec_batch))
    ones = jnp.ones((HEAD_DIM,), F32)

    yp = x_prompt.reshape(batch * seq, d)
    ys = x_sample.reshape(dec_batch * n_new, d)
    pwk = [[] for _ in range(N_GROUPS)]
    pwv = [[] for _ in range(N_GROUPS)]
    swk = [[] for _ in range(N_GROUPS)]
    swv = [[] for _ in range(N_GROUPS)]
    pfk, pfv, pfl, sfk, sfv, sfl = [], [], [], [], [], []

    def part_p(parts, p):
        return parts[:, p * d:(p + 1) * d].reshape(batch, seq, heads, HEAD_DIM)

    def part_s(parts, p):
        return parts[:, p * d:(p + 1) * d].reshape(dec_batch, n_new, heads, HEAD_DIM)

    for layer in range(depth):
        li = layer // 2
        if layer % 2 == 0:
            hp = _prenorm(yp, a_norm[li])
            hs = _prenorm(ys, a_norm[li])
            rows = []
            for g in range(N_GROUPS):
                rows += [a_q_gain[li, g], a_k_gain[li, g], ones]
            gains = jnp.stack(rows + [ones]).reshape(3 * N_GROUPS + 1, 1, HEAD_DIM)
            parts_p, parts_s = _in_proj(hp, hs, a_w_in, li, gains, rope_p, rope_s,
                                        n_out=a_w_in.shape[2], normed_fn=_a_normed, rope=True)
            zp = _attn_a_prompt(parts_p, batch=batch, seq=seq, heads=heads)
            zs = _attn_a_sample(parts_s, win_caches, li, batch=dec_batch, n_new=n_new, heads=heads)
            yp, ys = _out_proj(zp, zs, a_w_out, li, yp, ys)
            for g, (win, _) in enumerate(A_GROUPS):
                keep = min(win, seq)
                pwk[g].append(part_p(parts_p, 3 * g + 1)[:, seq - keep:])
                pwv[g].append(part_p(parts_p, 3 * g + 2)[:, seq - keep:])
                ck, cv = win_caches[g]
                swk[g].append(jnp.concatenate([ck[li][:, n_new:], part_s(parts_s, 3 * g + 1)], axis=1))
                swv[g].append(jnp.concatenate([cv[li][:, n_new:], part_s(parts_s, 3 * g + 2)], axis=1))
        else:
            hp = _prenorm(yp, f_norm[li])
            hs = _prenorm(ys, f_norm[li])
            gains = jnp.stack([f_q_gain[li], f_k_gain[li], ones, ones]).reshape(4, 1, HEAD_DIM)
            parts_p, parts_s = _in_proj(hp, hs, f_w_in, li, gains, rope_p, rope_s,
                                        n_out=4 * d, normed_fn=_f_normed, rope=False)
            w_f = f_w_in[li, :, 4 * d:]
            lf_p, _, c_t = _forget(hp, w_f, f_b_forget[li], batch=batch)
            lf_s = _forget_rows(hs, w_f, f_b_forget[li])
            zp = _attn_f_prompt(parts_p, c_t, batch=batch, seq=seq, heads=heads)
            c_past, tot = _page_prefix(fox_lft, page_table, li)
            zs = _attn_f_sample(parts_s, lf_s, c_past, tot, cache_fox_k, cache_fox_v, page_table, li,
                                batch=dec_batch, n_new=n_new, heads=heads)
            yp, ys = _out_proj(zp, zs, f_w_out, li, yp, ys)
            pfk.append(part_p(parts_p, 1))
            pfv.append(part_p(parts_p, 2))
            pfl.append(lf_p.reshape(batch, seq, heads))
            sfk.append(part_s(parts_s, 1))
            sfv.append(part_s(parts_s, 2))
            sfl.append(lf_s.reshape(dec_batch, n_new, heads))

    outs = [yp.reshape(batch, seq, d), ys.reshape(dec_batch, n_new, d)]
    for g in range(N_GROUPS):
        outs += [jnp.stack(pwk[g]), jnp.stack(pwv[g])]
    for g in range(N_GROUPS):
        outs += [jnp.stack(swk[g]), jnp.stack(swv[g])]
    outs += [jnp.stack(pfk), jnp.stack(pfv), jnp.stack(pfl), jnp.stack(sfk), jnp.stack(sfv), jnp.stack(sfl)]
    return tuple(outs)
```

```python
import functools

import jax
import jax.numpy as jnp
from jax import lax
from jax.experimental import pallas as pl
from jax.experimental.pallas import tpu as pltpu

HEAD_DIM = 128
A_GROUPS = ((128, 1), (512, 4), (2048, 16))
N_GROUPS = len(A_GROUPS)
ROPE_THETA = 10000.0
EPS = 1e-6
PAGES_PER_STEP = 4
PREFIX_PAGES_PER_STEP = 8

VMEM_LIMIT_BYTES = 60 * 1024 * 1024
PROJ_TM = 512
PROJ_TN = 1024
FOX_TQ = 256

F32 = jnp.float32
BF16 = jnp.bfloat16
NEG_INF = float("-inf")


def _cparams(*sem):
    return pltpu.CompilerParams(dimension_semantics=sem, vmem_limit_bytes=VMEM_LIMIT_BYTES)


def _prenorm_kernel(x_ref, g_ref, o_ref):
    x = x_ref[...]
    ms = jnp.mean(x * x, axis=-1, keepdims=True)
    o_ref[...] = (x * lax.rsqrt(ms + EPS) * g_ref[...]).astype(o_ref.dtype)


def _prenorm(x, g):
    m, d = x.shape
    tm = min(m, 512)
    return pl.pallas_call(
        _prenorm_kernel,
        grid=(m // tm,),
        in_specs=[pl.BlockSpec((tm, d), lambda i: (i, 0)), pl.BlockSpec((1, d), lambda i: (0, 0))],
        out_specs=pl.BlockSpec((tm, d), lambda i: (i, 0)),
        out_shape=jax.ShapeDtypeStruct((m, d), BF16),
        compiler_params=_cparams("arbitrary"),
        name="prenorm",
    )(x, g.reshape(1, d))


def _head_norm_rope(acc, gain, cos, sin, out_ref):
    for hh in range(acc.shape[1] // HEAD_DIM):
        sl = slice(hh * HEAD_DIM, (hh + 1) * HEAD_DIM)
        y = acc[:, sl]
        ms = jnp.mean(y * y, axis=-1, keepdims=True)
        y = y * lax.rsqrt(ms + EPS) * gain
        if cos is not None:
            y = y * cos + pltpu.roll(y, HEAD_DIM // 2, 1) * sin
        out_ref[:, sl] = y


def _in_proj_kernel(xp_ref, xs_ref, w_ref, gain_ref, cp_ref, sp_ref, cs_ref, ss_ref,
                    op_ref, os_ref, wbf_ref, *, normed_fn, tiles_per_part, rope):
    j = pl.program_id(0)
    m = pl.program_id(1)

    @pl.when(m == 0)
    def _():
        wbf_ref[...] = w_ref[...].astype(BF16)

    normed = normed_fn(j // tiles_per_part)
    plain = jnp.logical_not(normed)
    gain = gain_ref[...]

    def emit(x_ref, c_ref, s_ref, out_ref):
        acc = jnp.dot(x_ref[...], wbf_ref[...], preferred_element_type=F32)

        @pl.when(normed)
        def _():
            if rope:
                _head_norm_rope(acc, gain, c_ref[...], s_ref[...], out_ref)
            else:
                _head_norm_rope(acc, gain, None, None, out_ref)

        @pl.when(plain)
        def _():
            out_ref[...] = acc

    emit(xp_ref, cp_ref, sp_ref, op_ref)

    @pl.when(m == pl.num_programs(1) - 1)
    def _():
        emit(xs_ref, cs_ref, ss_ref, os_ref)


def _in_proj(hp, hs, w, layer, gains, rope_p, rope_s, *, n_out, normed_fn, rope):
    p_rows, k = hp.shape
    s_rows = hs.shape[0]
    tm = min(PROJ_TM, p_rows)
    tn = min(PROJ_TN, k)
    tiles_per_part = k // tn
    cos_p, sin_p = rope_p
    cos_s, sin_s = rope_s
    pos_tiles = cos_p.shape[0] // tm
    grid = (n_out // tn, p_rows // tm)
    kern = functools.partial(_in_proj_kernel, normed_fn=normed_fn, tiles_per_part=tiles_per_part, rope=rope)
    return pl.pallas_call(
        kern,
        grid=grid,
        in_specs=[
            pl.BlockSpec((tm, k), lambda j, m: (m, 0)),
            pl.BlockSpec((s_rows, k), lambda j, m: (0, 0)),
            pl.BlockSpec((None, k, tn), lambda j, m: (layer, 0, j), pipeline_mode=pl.Buffered(1)),
            pl.BlockSpec((None, 1, HEAD_DIM), lambda j, m: (j // tiles_per_part, 0, 0)),
            pl.BlockSpec((tm, HEAD_DIM), lambda j, m: (m % pos_tiles, 0)),
            pl.BlockSpec((tm, HEAD_DIM), lambda j, m: (m % pos_tiles, 0)),
            pl.BlockSpec((s_rows, HEAD_DIM), lambda j, m: (0, 0)),
            pl.BlockSpec((s_rows, HEAD_DIM), lambda j, m: (0, 0)),
        ],
        out_specs=[
            pl.BlockSpec((tm, tn), lambda j, m: (m, j)),
            pl.BlockSpec((s_rows, tn), lambda j, m: (0, j)),
        ],
        out_shape=[
            jax.ShapeDtypeStruct((p_rows, n_out), F32),
            jax.ShapeDtypeStruct((s_rows, n_out), F32),
        ],
        scratch_shapes=[pltpu.VMEM((k, tn), BF16)],
        compiler_params=_cparams("arbitrary", "arbitrary"),
        name="in_proj",
    )(hp, hs, w, gains, cos_p, sin_p, cos_s, sin_s)


def _out_proj_kernel(zp_ref, zs_ref, w_ref, rp_ref, rs_ref, op_ref, os_ref, wbf_ref):
    m = pl.program_id(1)

    @pl.when(m == 0)
    def _():
        wbf_ref[...] = w_ref[...].astype(BF16)

    op_ref[...] = rp_ref[...] + jnp.dot(zp_ref[...], wbf_ref[...], preferred_element_type=F32)

    @pl.when(m == pl.num_programs(1) - 1)
    def _():
        os_ref[...] = rs_ref[...] + jnp.dot(zs_ref[...].astype(BF16), wbf_ref[...],
                                            preferred_element_type=F32)


def _out_proj(zp, zs, w, layer, xp, xs):
    p_rows, k = zp.shape
    s_rows = zs.shape[0]
    n = w.shape[2]
    tm = min(PROJ_TM, p_rows)
    tn = min(PROJ_TN, n)
    return pl.pallas_call(
        _out_proj_kernel,
        grid=(n // tn, p_rows // tm),
        in_specs=[
            pl.BlockSpec((tm, k), lambda j, m: (m, 0)),
            pl.BlockSpec((s_rows, k), lambda j, m: (0, 0)),
            pl.BlockSpec((None, k, tn), lambda j, m: (layer, 0, j), pipeline_mode=pl.Buffered(1)),
            pl.BlockSpec((tm, tn), lambda j, m: (m, j)),
            pl.BlockSpec((s_rows, tn), lambda j, m: (0, j)),
        ],
        out_specs=[
            pl.BlockSpec((tm, tn), lambda j, m: (m, j)),
            pl.BlockSpec((s_rows, tn), lambda j, m: (0, j)),
        ],
        out_shape=[
            jax.ShapeDtypeStruct((p_rows, n), F32),
            jax.ShapeDtypeStruct((s_rows, n), F32),
        ],
        scratch_shapes=[pltpu.VMEM((k, tn), BF16)],
        compiler_params=_cparams("arbitrary", "arbitrary"),
        name="out_proj",
    )(zp, zs, w, xp, xs)


def _to_native_kernel(*refs, n_layers, n_arrays):
    srcs, dsts = refs[:n_layers * n_arrays], refs[n_layers * n_arrays:]
    for layer in range(n_layers):
        @pl.when(pl.program_id(0) == layer)
        def _():
            for a, dst in enumerate(dsts):
                src = srcs[layer * n_arrays + a]
                for hh in range(dst.shape[1]):
                    dst[:, hh, :] = src[:, hh * HEAD_DIM:(hh + 1) * HEAD_DIM]


def _to_native(parts_by_layer, part_ids, *, heads):
    n_layers = len(parts_by_layer)
    m = parts_by_layer[0].shape[0]
    hg = min(8, heads)
    n_hg = heads // hg
    tm = min(PROJ_TM, m)
    n_m = m // tm
    n = len(part_ids)

    def src_spec(layer, part):
        def index_map(l, i, c):
            row = jnp.where(l < layer, 0, jnp.where(l > layer, n_m - 1, i))
            col = jnp.where(l < layer, 0, jnp.where(l > layer, n_hg - 1, c))
            return row, part * n_hg + col
        return pl.BlockSpec((tm, hg * HEAD_DIM), index_map)

    return pl.pallas_call(
        functools.partial(_to_native_kernel, n_layers=n_layers, n_arrays=n),
        grid=(n_layers, n_m, n_hg),
        in_specs=[src_spec(layer, p) for layer in range(n_layers) for p in part_ids],
        out_specs=[pl.BlockSpec((None, tm, hg, HEAD_DIM), lambda l, i, c: (l, i, c, 0))] * n,
        out_shape=[jax.ShapeDtypeStruct((n_layers, m, heads, HEAD_DIM), F32)] * n,
        compiler_params=_cparams("arbitrary", "arbitrary", "arbitrary"),
        name="to_native",
    )(*[parts for parts in parts_by_layer for _ in part_ids])


def _shift_caches_kernel(*refs, n_new):
    n = (len(refs) - 2) // 3
    srcs, news, dsts = refs[:n], refs[n:2 * n], refs[2 * n:3 * n]
    old_sems, new_sems = refs[3 * n:]

    def copies(i, layer, b):
        keep = srcs[i].shape[2] - n_new
        out = [pltpu.make_async_copy(news[i].at[layer, b], dsts[i].at[layer, b, pl.ds(keep, n_new)],
                                     new_sems.at[i, layer, b])]
        if keep > 0:
            out.append(pltpu.make_async_copy(srcs[i].at[layer, b, pl.ds(n_new, keep)],
                                             dsts[i].at[layer, b, pl.ds(0, keep)],
                                             old_sems.at[i, layer, b]))
        return out

    windows = [(i, layer, b) for i in range(n)
               for layer in range(srcs[i].shape[0]) for b in range(srcs[i].shape[1])]
    for w in windows:
        for cp in copies(*w):
            cp.start()
    for w in windows:
        for cp in copies(*w):
            cp.wait()


def _shift_caches(caches, new_rows, n_new):
    n_layers, batch = caches[0].shape[:2]
    n = len(caches)
    return pl.pallas_call(
        functools.partial(_shift_caches_kernel, n_new=n_new),
        in_specs=[pl.BlockSpec(memory_space=pl.ANY)] * (2 * n),
        out_specs=[pl.BlockSpec(memory_space=pl.ANY)] * n,
        out_shape=[jax.ShapeDtypeStruct(c.shape, c.dtype) for c in caches],
        scratch_shapes=[pltpu.SemaphoreType.DMA((n, n_layers, batch)),
                        pltpu.SemaphoreType.DMA((n, n_layers, batch))],
        name="shift_caches",
    )(*caches, *new_rows)


def _silu(x):
    return x * (1.0 / (1.0 + jnp.exp(-x)))


def _qk(q, k):
    return lax.dot_general(q.astype(BF16), k.astype(BF16), (((1,), (1,)), ((), ())),
                           preferred_element_type=F32)


def _pv(p, v):
    return jnp.dot(p.astype(BF16), v.astype(BF16), preferred_element_type=F32)


def _softmax_pv(s, v):
    mx = jnp.max(s, axis=-1, keepdims=True)
    p = jnp.exp(s - mx)
    den = jnp.sum(p, axis=-1, keepdims=True)
    o = _pv(p, v) * (1.0 / den)
    return o, mx + jnp.log(den)


def _merge_groups(os_, lses):
    mx = functools.reduce(jnp.maximum, lses)
    ws = [jnp.exp(l - mx) for l in lses]
    tot = functools.reduce(lambda a, b: a + b, ws)
    acc = functools.reduce(lambda a, b: a + b, [w * o for w, o in zip(ws, os_)])
    return acc * (1.0 / tot)


def _attn_a_prompt_kernel(*refs, seq):
    qkv = refs[:3 * N_GROUPS]
    gate_ref = refs[3 * N_GROUPS]
    z_ref = refs[3 * N_GROUPS + 1]
    o_scr, lse_scr = refs[3 * N_GROUPS + 2:]
    scale = HEAD_DIM ** -0.5

    for g, (win, dil) in enumerate(A_GROUPS):
        q_ref, k_ref, v_ref = qkv[3 * g:3 * g + 3]
        steps = win // dil
        n_blocks = (seq // dil) // steps
        r = lax.broadcasted_iota(jnp.int32, (steps, steps), 0)
        c = lax.broadcasted_iota(jnp.int32, (steps, steps), 1)
        first_mask = c <= r
        r2 = lax.broadcasted_iota(jnp.int32, (steps, 2 * steps), 0)
        c2 = lax.broadcasted_iota(jnp.int32, (steps, 2 * steps), 1)
        rest_mask = (c2 >= r2) & (c2 <= r2 + steps)

        def rows(start, n):
            return pl.ds(start, n, stride=dil) if dil > 1 else pl.ds(start, n)

        for phase in range(dil):
            for blk in range(n_blocks):
                q_rows = rows(phase + dil * steps * blk, steps)
                if blk == 0:
                    k_rows, mask = q_rows, first_mask
                else:
                    k_rows, mask = rows(phase + dil * steps * (blk - 1), 2 * steps), rest_mask
                s = _qk(q_ref[q_rows, :], k_ref[k_rows, :]) * scale
                s = jnp.where(mask, s, NEG_INF)
                o, lse = _softmax_pv(s, v_ref[k_rows, :])
                o_scr[g, q_rows, :] = o
                lse_scr[g, q_rows, :] = jnp.broadcast_to(lse, (steps, HEAD_DIM))

    merged = _merge_groups([o_scr[g] for g in range(N_GROUPS)], [lse_scr[g] for g in range(N_GROUPS)])
    z_ref[...] = (merged * _silu(gate_ref[...])).astype(z_ref.dtype)


def _attn_a_prompt(parts, *, batch, seq, heads):
    def col(part):
        return pl.BlockSpec((seq, HEAD_DIM), lambda b, h: (b, part * heads + h))

    n_in = 3 * N_GROUPS + 1
    return pl.pallas_call(
        functools.partial(_attn_a_prompt_kernel, seq=seq),
        grid=(batch, heads),
        in_specs=[col(p) for p in range(n_in)],
        out_specs=pl.BlockSpec((seq, HEAD_DIM), lambda b, h: (b, h)),
        out_shape=jax.ShapeDtypeStruct((batch * seq, heads * HEAD_DIM), BF16),
        scratch_shapes=[pltpu.VMEM((N_GROUPS, seq, HEAD_DIM), F32),
                        pltpu.VMEM((N_GROUPS, seq, HEAD_DIM), F32)],
        compiler_params=_cparams("arbitrary", "arbitrary"),
        name="attn_a_prompt",
    )(*([parts] * n_in))


def _attn_a_sample_kernel(*refs, n_new):
    new = refs[:3 * N_GROUPS]
    gate_ref = refs[3 * N_GROUPS]
    caches = refs[3 * N_GROUPS + 1:5 * N_GROUPS + 1]
    z_ref = refs[5 * N_GROUPS + 1]
    scale = HEAD_DIM ** -0.5
    ht = gate_ref.shape[1] // HEAD_DIM
    rows_t = ht * n_new
    pad = jnp.zeros((HEAD_DIM - rows_t, HEAD_DIM), BF16)

    def tile_rows(ref):
        return jnp.concatenate([ref[:, hs * HEAD_DIM:(hs + 1) * HEAD_DIM] for hs in range(ht)], axis=0)

    outs, lses = [], []
    for g, (win, dil) in enumerate(A_GROUPS):
        q_ref, kn_ref, vn_ref = new[3 * g:3 * g + 3]
        kc_ref, vc_ref = caches[2 * g:2 * g + 2]
        buf = kc_ref.shape[0]
        n_cached = buf * ht
        k_all = jnp.concatenate([kc_ref[...].reshape(n_cached, HEAD_DIM).astype(BF16),
                                 tile_rows(kn_ref).astype(BF16), pad], axis=0)
        v_all = jnp.concatenate([vc_ref[...].reshape(n_cached, HEAD_DIM).astype(BF16),
                                 tile_rows(vn_ref).astype(BF16), pad], axis=0)
        s = _qk(tile_rows(q_ref), k_all) * scale
        r = lax.broadcasted_iota(jnp.int32, s.shape, 0)
        c = lax.broadcasted_iota(jnp.int32, s.shape, 1)
        cached = c < n_cached
        c_new = c - n_cached
        key_head = jnp.where(cached, lax.rem(c, ht), lax.div(c_new, n_new))
        key_pos = jnp.where(cached, lax.div(c, ht), buf + lax.rem(c_new, n_new))
        back = buf + lax.rem(r, n_new) - key_pos
        valid = ((key_head == lax.div(r, n_new)) & (back >= 0) & (back <= win)
                 & ((back & (dil - 1)) == 0))
        s = jnp.where(valid, s, NEG_INF)
        o, lse = _softmax_pv(s, v_all)
        outs.append(o)
        lses.append(lse)
    merged = _merge_groups(outs, lses)
    for hs in range(ht):
        cols = slice(hs * HEAD_DIM, (hs + 1) * HEAD_DIM)
        z_ref[:, cols] = merged[hs * n_new:(hs + 1) * n_new, :] * _silu(gate_ref[:, cols])


def _attn_a_sample(parts, caches, layer, *, batch, n_new, heads):
    hg = min(8, heads)
    n_hg = heads // hg

    def col(part):
        return pl.BlockSpec((n_new, hg * HEAD_DIM), lambda b, h: (b, part * n_hg + h))

    def cache_spec(c):
        return pl.BlockSpec((None, None, c.shape[2], hg, HEAD_DIM), lambda b, h: (layer, b, 0, h, 0))

    n_in = 3 * N_GROUPS + 1
    flat = [c for kv in caches for c in kv]
    return pl.pallas_call(
        functools.partial(_attn_a_sample_kernel, n_new=n_new),
        grid=(batch, n_hg),
        in_specs=[col(p) for p in range(n_in)] + [cache_spec(c) for c in flat],
        out_specs=pl.BlockSpec((n_new, hg * HEAD_DIM), lambda b, h: (b, h)),
        out_shape=jax.ShapeDtypeStruct((batch * n_new, heads * HEAD_DIM), F32),
        compiler_params=_cparams("arbitrary", "arbitrary"),
        name="attn_a_sample",
    )(*([parts] * n_in), *flat)


def _log_sigmoid(x):
    return jnp.minimum(x, 0.0) - jnp.log(1.0 + jnp.exp(-jnp.abs(x)))


def _cumsum_lanes(x):
    n = x.shape[-1]
    lane = lax.broadcasted_iota(jnp.int32, x.shape, x.ndim - 1)
    shift = 1
    while shift < n:
        x = x + jnp.where(lane >= shift, pltpu.roll(x, shift, x.ndim - 1), 0.0)
        shift *= 2
    return x


def _forget_kernel(h_ref, w_ref, wt_ref, b_ref, bt_ref, lf_ref, lft_ref, ct_ref):
    h = h_ref[...]
    lf_ref[...] = _log_sigmoid(jnp.dot(h, w_ref[...].astype(BF16), preferred_element_type=F32) + b_ref[...])
    lft = _log_sigmoid(_qk(wt_ref[...], h) + bt_ref[...])
    lft_ref[...] = lft
    ct_ref[...] = _cumsum_lanes(lft)


def _forget_rows_kernel(h_ref, w_ref, b_ref, lf_ref):
    lf_ref[...] = _log_sigmoid(
        jnp.dot(h_ref[...], w_ref[...].astype(BF16), preferred_element_type=F32) + b_ref[...])


def _forget_rows(h, w_f, b_f):
    m, d = h.shape
    nh = w_f.shape[1]
    return pl.pallas_call(
        _forget_rows_kernel,
        grid=(1,),
        in_specs=[pl.BlockSpec((m, d), lambda i: (0, 0)),
                  pl.BlockSpec((d, nh), lambda i: (0, 0)),
                  pl.BlockSpec((1, nh), lambda i: (0, 0))],
        out_specs=pl.BlockSpec((m, nh), lambda i: (0, 0)),
        out_shape=jax.ShapeDtypeStruct((m, nh), F32),
        compiler_params=_cparams("arbitrary"),
        name="forget_rows",
    )(h, w_f, b_f.reshape(1, nh))


def _forget(h, w_f, b_f, *, batch):
    m, d = h.shape
    rows = m // batch
    nh = w_f.shape[1]
    t_shape = jax.ShapeDtypeStruct((batch, nh, rows), F32)
    t_spec = pl.BlockSpec((None, nh, rows), lambda b: (b, 0, 0))
    return pl.pallas_call(
        _forget_kernel,
        grid=(batch,),
        in_specs=[
            pl.BlockSpec((rows, d), lambda b: (b, 0)),
            pl.BlockSpec((d, nh), lambda b: (0, 0)),
            pl.BlockSpec((nh, d), lambda b: (0, 0)),
            pl.BlockSpec((1, nh), lambda b: (0, 0)),
            pl.BlockSpec((nh, 1), lambda b: (0, 0)),
        ],
        out_specs=[pl.BlockSpec((rows, nh), lambda b: (b, 0)), t_spec, t_spec],
        out_shape=[jax.ShapeDtypeStruct((m, nh), F32), t_shape, t_shape],
        compiler_params=_cparams("arbitrary"),
        name="forget",
    )(h, w_f, w_f.T, b_f.reshape(1, nh), b_f.reshape(nh, 1))


def _attn_f_prompt_kernel(q_ref, k_ref, v_ref, gate_ref, c_ref, z_ref, *, seq, tq):
    scale = HEAD_DIM ** -0.5
    hh = pl.program_id(1) % c_ref.shape[0]
    c_row = c_ref[pl.ds(hh, 1), :]
    r = lax.broadcasted_iota(jnp.int32, (tq, tq), 0)
    c = lax.broadcasted_iota(jnp.int32, (tq, tq), 1)
    diag_mask = c <= r

    for qi in range(seq // tq):
        q_rows = slice(qi * tq, (qi + 1) * tq)
        q = q_ref[q_rows, :].astype(BF16)
        c_q = jnp.broadcast_to(c_row[:, q_rows], (tq, tq)).T
        m_i = jnp.full((tq, 1), NEG_INF, F32)
        l_i = jnp.zeros((tq, 1), F32)
        acc = jnp.zeros((tq, HEAD_DIM), F32)
        for kj in range(qi + 1):
            k_rows = slice(kj * tq, (kj + 1) * tq)
            s = _qk(q, k_ref[k_rows, :]) * scale + (c_q - c_row[:, k_rows])
            if kj == qi:
                s = jnp.where(diag_mask, s, NEG_INF)
            m_new = jnp.maximum(m_i, jnp.max(s, axis=-1, keepdims=True))
            alpha = jnp.exp(m_i - m_new)
            p = jnp.exp(s - m_new)
            l_i = alpha * l_i + jnp.sum(p, axis=-1, keepdims=True)
            acc = alpha * acc + _pv(p, v_ref[k_rows, :])
            m_i = m_new
        o = acc * (1.0 / l_i)
        z_ref[q_rows, :] = (o * _silu(gate_ref[q_rows, :])).astype(z_ref.dtype)


def _attn_f_prompt(parts, c_t, *, batch, seq, heads):
    def col(part):
        return pl.BlockSpec((seq, HEAD_DIM), lambda b, h: (b, part * heads + h))

    tq = min(FOX_TQ, seq)
    c_rows = min(8, heads)
    return pl.pallas_call(
        functools.partial(_attn_f_prompt_kernel, seq=seq, tq=tq),
        grid=(batch, heads),
        in_specs=[col(0), col(1), col(2), col(3),
                  pl.BlockSpec((None, c_rows, seq), lambda b, h: (b, h // c_rows, 0))],
        out_specs=pl.BlockSpec((seq, HEAD_DIM), lambda b, h: (b, h)),
        out_shape=jax.ShapeDtypeStruct((batch * seq, heads * HEAD_DIM), BF16),
        compiler_params=_cparams("arbitrary", "arbitrary"),
        name="attn_f_prompt",
    )(parts, parts, parts, parts, c_t)


def _page_prefix_kernel(*refs, pages_per_step):
    g = pages_per_step
    lft_refs = refs[1:1 + g]
    c_ref, tot_ref, carry_ref = refs[1 + g:]
    j = pl.program_id(1)
    page = carry_ref.shape[1]

    @pl.when(j == 0)
    def _():
        carry_ref[...] = jnp.zeros_like(carry_ref)

    carry = carry_ref[...]
    for i in range(g):
        c = _cumsum_lanes(lft_refs[i][...]) + carry
        c_ref[:, i * page:(i + 1) * page] = c
        carry = jnp.broadcast_to(c[:, -1:], c.shape)
    carry_ref[...] = carry
    tot_ref[...] = carry


def _page_prefix(lft_pages, page_table, layer):
    _, _, nh, page = lft_pages.shape
    batch, n_pages = page_table.shape
    g = PREFIX_PAGES_PER_STEP if n_pages % PREFIX_PAGES_PER_STEP == 0 else 1

    def page_spec(i):
        return pl.BlockSpec((None, None, nh, page), lambda b, j, pt: (layer, pt[b, j * g + i], 0, 0))

    return pl.pallas_call(
        functools.partial(_page_prefix_kernel, pages_per_step=g),
        grid_spec=pltpu.PrefetchScalarGridSpec(
            num_scalar_prefetch=1,
            grid=(batch, n_pages // g),
            in_specs=[page_spec(i) for i in range(g)],
            out_specs=[pl.BlockSpec((None, nh, g * page), lambda b, j, pt: (b, 0, j)),
                       pl.BlockSpec((None, nh, page), lambda b, j, pt: (b, 0, 0))],
            scratch_shapes=[pltpu.VMEM((nh, page), F32)],
        ),
        out_shape=[jax.ShapeDtypeStruct((batch, nh, n_pages * page), F32),
                   jax.ShapeDtypeStruct((batch, nh, page), F32)],
        compiler_params=_cparams("arbitrary", "arbitrary"),
        name="page_prefix",
    )(page_table, *([lft_pages] * g))


def _attn_f_sample_kernel(*refs, heads, n_new, page, pages_per_step):
    g = pages_per_step
    pt_ref = refs[0]
    del pt_ref
    q_ref, kn_ref, vn_ref, gate_ref, lfn_ref, c_ref, tot_ref = refs[1:8]
    k_refs = refs[8:8 + g]
    v_refs = refs[8 + g:8 + 2 * g]
    z_ref = refs[8 + 2 * g]
    q_scr, m_scr, l_scr, acc_scr = refs[9 + 2 * g:]
    j = pl.program_id(1)
    scale = HEAD_DIM ** -0.5
    n_tiles, rows_t, _ = q_scr.shape
    ht = rows_t // n_new

    def head_cols(h):
        return slice(h * HEAD_DIM, (h + 1) * HEAD_DIM)

    def tile_rows(ref, tile):
        return jnp.concatenate([ref[:, head_cols(tile * ht + hs)] for hs in range(ht)], axis=0)

    @pl.when(j == 0)
    def _():
        m_scr[...] = jnp.full_like(m_scr, NEG_INF)
        l_scr[...] = jnp.zeros_like(l_scr)
        acc_scr[...] = jnp.zeros_like(acc_scr)
        for tile in range(n_tiles):
            q_scr[tile] = tile_rows(q_ref, tile).astype(BF16)

    lfn = lfn_ref[...]
    row = lax.broadcasted_iota(jnp.int32, lfn.shape, 0)
    shift = 1
    while shift < n_new:
        lfn = lfn + jnp.where(row >= shift, pltpu.roll(lfn, shift, 0), 0.0)
        shift *= 2
    tot = tot_ref[...]

    def tile_col(tile, with_total):
        cols = []
        for h in range(tile * ht, (tile + 1) * ht):
            cols.append(lfn[:, h:h + 1] + tot[h:h + 1, 0:1] if with_total else lfn[:, h:h + 1])
        return jnp.concatenate(cols, axis=0)

    def attend(tile, kb, vb, bias, mask):
        s = lax.dot_general(q_scr[tile], kb, (((1,), (1,)), ((), ())), preferred_element_type=F32)
        s = jnp.where(mask, s * scale + bias, NEG_INF)
        m_old = m_scr[tile]
        m_new = jnp.maximum(m_old, jnp.max(s, axis=-1, keepdims=True))
        alpha = jnp.exp(m_old - m_new)
        p = jnp.exp(s - m_new)
        l_scr[tile] = alpha * l_scr[tile] + jnp.sum(p, axis=-1, keepdims=True)
        acc_scr[tile] = alpha * acc_scr[tile] + jnp.dot(p.astype(BF16), vb, preferred_element_type=F32)
        m_scr[tile] = m_new

    def tile_keys(page_refs, tile):
        blocks = [r[:, tile * ht:(tile + 1) * ht, :].reshape(page * ht, HEAD_DIM) for r in page_refs]
        return (jnp.concatenate(blocks, axis=0) if len(blocks) > 1 else blocks[0]).astype(BF16)

    n_keys = g * page * ht
    key_head = lax.rem(lax.broadcasted_iota(jnp.int32, (rows_t, n_keys), 1), ht)
    own_head = key_head == lax.div(lax.broadcasted_iota(jnp.int32, (rows_t, n_keys), 0), n_new)
    for tile in range(n_tiles):
        attend(tile, tile_keys(k_refs, tile), tile_keys(v_refs, tile),
               tile_col(tile, True) - c_ref[tile], own_head)

    @pl.when(j == pl.num_programs(1) - 1)
    def _():
        pad = jnp.zeros((HEAD_DIM - rows_t, HEAD_DIM), F32)
        r = lax.broadcasted_iota(jnp.int32, (rows_t, HEAD_DIM), 0)
        c = lax.broadcasted_iota(jnp.int32, (rows_t, HEAD_DIM), 1)
        mask = (lax.div(c, n_new) == lax.div(r, n_new)) & (lax.rem(c, n_new) <= lax.rem(r, n_new))
        n_idx = lax.broadcasted_iota(jnp.int32, (n_new, HEAD_DIM), 0)
        c_idx = lax.broadcasted_iota(jnp.int32, (n_new, HEAD_DIM), 1)
        for tile in range(n_tiles):
            kn = jnp.concatenate([tile_rows(kn_ref, tile), pad], axis=0).astype(BF16)
            vn = jnp.concatenate([tile_rows(vn_ref, tile), pad], axis=0).astype(BF16)
            per_token = jnp.zeros((n_new, HEAD_DIM), F32)
            for hs in range(ht):
                h = tile * ht + hs
                per_token = jnp.where(lax.div(c_idx, n_new) == hs, lfn[:, h:h + 1], per_token)
            c_s = jnp.sum(jnp.where(lax.rem(c_idx, n_new) == n_idx, per_token, 0.0), axis=0, keepdims=True)
            attend(tile, kn, vn, tile_col(tile, False) - c_s, mask)
            o = acc_scr[tile] * (1.0 / l_scr[tile])
            for hs in range(ht):
                cols = head_cols(tile * ht + hs)
                z_ref[:, cols] = o[hs * n_new:(hs + 1) * n_new, :] * _silu(gate_ref[:, cols])


def _attn_f_sample(parts, lf_new, c_past, tot, k_pages, v_pages, page_table, layer, *, batch, n_new, heads):
    page = k_pages.shape[2]
    n_pages = page_table.shape[1]
    g = PAGES_PER_STEP if n_pages % PAGES_PER_STEP == 0 else 1
    width = heads * HEAD_DIM
    ht = min(8, heads)
    n_tiles = heads // ht
    rows_t = ht * n_new
    c_tiles = c_past.reshape(batch, n_tiles, ht, n_pages * page)
    c_tiles = jnp.swapaxes(c_tiles, 2, 3).reshape(batch, n_tiles, 1, n_pages * page * ht)

    def col(part):
        return pl.BlockSpec((n_new, width), lambda b, j, pt: (b, part))

    def page_spec(i):
        return pl.BlockSpec((None, None, page, heads, HEAD_DIM),
                            lambda b, j, pt: (layer, pt[b, j * g + i], 0, 0, 0))

    kern = functools.partial(_attn_f_sample_kernel, heads=heads, n_new=n_new, page=page, pages_per_step=g)
    return pl.pallas_call(
        kern,
        grid_spec=pltpu.PrefetchScalarGridSpec(
            num_scalar_prefetch=1,
            grid=(batch, n_pages // g),
            in_specs=[col(0), col(1), col(2), col(3),
                      pl.BlockSpec((n_new, heads), lambda b, j, pt: (b, 0)),
                      pl.BlockSpec((None, n_tiles, 1, g * page * ht), lambda b, j, pt: (b, 0, 0, j)),
                      pl.BlockSpec((None, heads, page), lambda b, j, pt: (b, 0, 0))]
            + [page_spec(i) for i in range(g)] + [page_spec(i) for i in range(g)],
            out_specs=pl.BlockSpec((n_new, width), lambda b, j, pt: (b, 0)),
            scratch_shapes=[pltpu.VMEM((n_tiles, rows_t, HEAD_DIM), BF16),
                            pltpu.VMEM((n_tiles, rows_t, 1), F32),
                            pltpu.VMEM((n_tiles, rows_t, 1), F32),
                            pltpu.VMEM((n_tiles, rows_t, HEAD_DIM), F32)],
        ),
        out_shape=jax.ShapeDtypeStruct((batch * n_new, width), F32),
        compiler_params=_cparams("arbitrary", "arbitrary"),
        name="attn_f_sample",
    )(page_table, parts, parts, parts, parts, lf_new, c_tiles, tot,
      *([k_pages] * g), *([v_pages] * g))


def _rope_tables(pos):
    half = HEAD_DIM // 2
    freqs = ROPE_THETA ** (-jnp.arange(half, dtype=F32) * 2.0 / HEAD_DIM)
    ang = pos.astype(F32)[:, None] * freqs[None, :]
    cos, sin = jnp.cos(ang), jnp.sin(ang)
    return jnp.concatenate([cos, cos], axis=-1), jnp.concatenate([-sin, sin], axis=-1)


def _a_normed(part):
    return (part % 3 != 2) & (part != 3 * N_GROUPS)


def _f_normed(part):
    return part < 2


def kernel(x_prompt, x_sample, cache_win0_k, cache_win0_v, cache_win1_k, cache_win1_v, cache_win2_k, cache_win2_v,
           cache_fox_k, cache_fox_v, cache_fox_logf, page_table,
           a_norm, a_w_in, a_q_gain, a_k_gain, a_w_out,
           f_norm, f_w_in, f_b_forget, f_q_gain, f_k_gain, f_w_out):
    batch, seq, d = x_prompt.shape
    dec_batch, n_new, _ = x_sample.shape
    heads = d // HEAD_DIM
    depth = a_norm.shape[0] + f_norm.shape[0]
    past = page_table.shape[1] * cache_fox_k.shape[2]

    win_caches = [(cache_win0_k, cache_win0_v), (cache_win1_k, cache_win1_v), (cache_win2_k, cache_win2_v)]
    fox_lft = jnp.swapaxes(cache_fox_logf, 2, 3)

    rope_p = _rope_tables(jnp.arange(seq))
    rope_s = _rope_tables(past + jnp.tile(jnp.arange(n_new), dec_batch))
    ones = jnp.ones((HEAD_DIM,), F32)

    yp = x_prompt.reshape(batch * seq, d)
    ys = x_sample.reshape(dec_batch * n_new, d)
    pwk = [[] for _ in range(N_GROUPS)]
    pwv = [[] for _ in range(N_GROUPS)]
    sw_new = [[] for _ in range(2 * N_GROUPS)]
    pfl, sfk, sfv, sfl = [], [], [], []
    n_a, n_f = a_norm.shape[0], f_norm.shape[0]
    a_parts, f_parts = [], []
    full_ids = [p for g, (win, _) in enumerate(A_GROUPS) if win >= seq for p in (3 * g + 1, 3 * g + 2)]

    def part_p(parts, p):
        return parts[:, p * d:(p + 1) * d].reshape(batch, seq, heads, HEAD_DIM)

    def part_s(parts, p):
        return parts[:, p * d:(p + 1) * d].reshape(dec_batch, n_new, heads, HEAD_DIM)

    for layer in range(depth):
        li = layer // 2
        if layer % 2 == 0:
            hp = _prenorm(yp, a_norm[li])
            hs = _prenorm(ys, a_norm[li])
            rows = []
            for g in range(N_GROUPS):
                rows += [a_q_gain[li, g], a_k_gain[li, g], ones]
            gains = jnp.stack(rows + [ones]).reshape(3 * N_GROUPS + 1, 1, HEAD_DIM)
            parts_p, parts_s = _in_proj(hp, hs, a_w_in, li, gains, rope_p, rope_s,
                                        n_out=a_w_in.shape[2], normed_fn=_a_normed, rope=True)
            zp = _attn_a_prompt(parts_p, batch=batch, seq=seq, heads=heads)
            zs = _attn_a_sample(parts_s, win_caches, li, batch=dec_batch, n_new=n_new, heads=heads)
            yp, ys = _out_proj(zp, zs, a_w_out, li, yp, ys)
            a_parts.append(parts_p)
            for g, (win, _) in enumerate(A_GROUPS):
                keep = min(win, seq)
                if keep < seq:
                    pwk[g].append(part_p(parts_p, 3 * g + 1)[:, seq - keep:])
                    pwv[g].append(part_p(parts_p, 3 * g + 2)[:, seq - keep:])
                for kv in range(2):
                    sw_new[2 * g + kv].append(part_s(parts_s, 3 * g + 1 + kv))
        else:
            hp = _prenorm(yp, f_norm[li])
            hs = _prenorm(ys, f_norm[li])
            gains = jnp.stack([f_q_gain[li], f_k_gain[li], ones, ones]).reshape(4, 1, HEAD_DIM)
            parts_p, parts_s = _in_proj(hp, hs, f_w_in, li, gains, rope_p, rope_s,
                                        n_out=4 * d, normed_fn=_f_normed, rope=False)
            w_f = f_w_in[li, :, 4 * d:]
            lf_p, _, c_t = _forget(hp, w_f, f_b_forget[li], batch=batch)
            lf_s = _forget_rows(hs, w_f, f_b_forget[li])
            zp = _attn_f_prompt(parts_p, c_t, batch=batch, seq=seq, heads=heads)
            c_past, tot = _page_prefix(fox_lft, page_table, li)
            zs = _attn_f_sample(parts_s, lf_s, c_past, tot, cache_fox_k, cache_fox_v, page_table, li,
                                batch=dec_batch, n_new=n_new, heads=heads)
            yp, ys = _out_proj(zp, zs, f_w_out, li, yp, ys)
            f_parts.append(parts_p)
            pfl.append(lf_p.reshape(batch, seq, heads))
            sfk.append(part_s(parts_s, 1))
            sfv.append(part_s(parts_s, 2))
            sfl.append(lf_s.reshape(dec_batch, n_new, heads))

    def native(buf, n_layers):
        return buf.reshape(n_layers, batch, seq, heads, HEAD_DIM)

    outs = [yp.reshape(batch, seq, d), ys.reshape(dec_batch, n_new, d)]
    full = iter(_to_native(a_parts, full_ids, heads=heads) if full_ids else [])
    for g, (win, _) in enumerate(A_GROUPS):
        if win >= seq:
            outs += [native(next(full), n_a), native(next(full), n_a)]
        else:
            outs += [jnp.stack(pwk[g]), jnp.stack(pwv[g])]
    outs += _shift_caches([c for kv in win_caches for c in kv], [jnp.stack(rows) for rows in sw_new], n_new)
    pf_full = _to_native(f_parts, [1, 2], heads=heads)
    outs += [native(pf_full[0], n_f), native(pf_full[1], n_f),
             jnp.stack(pfl), jnp.stack(sfk), jnp.stack(sfv), jnp.stack(sfl)]
    return tuple(outs)
```

```python
import functools

import jax
import jax.numpy as jnp
from jax import lax
from jax.experimental import pallas as pl
from jax.experimental.pallas import tpu as pltpu

HEAD_DIM = 128
A_GROUPS = ((128, 1), (512, 4), (2048, 16))
N_GROUPS = len(A_GROUPS)
ROPE_THETA = 10000.0
EPS = 1e-6
PAGES_PER_STEP = 4
PREFIX_PAGES_PER_STEP = 8

VMEM_LIMIT_BYTES = 60 * 1024 * 1024
PROJ_TM = 512
PROJ_TN = 1024
FOX_TQ = 256

F32 = jnp.float32
BF16 = jnp.bfloat16
NEG_INF = float("-inf")


def _cparams(*sem):
    return pltpu.CompilerParams(dimension_semantics=sem, vmem_limit_bytes=VMEM_LIMIT_BYTES)


def _prenorm_kernel(x_ref, g_ref, o_ref):
    x = x_ref[...]
    ms = jnp.mean(x * x, axis=-1, keepdims=True)
    o_ref[...] = (x * lax.rsqrt(ms + EPS) * g_ref[...]).astype(o_ref.dtype)


def _prenorm(x, g):
    m, d = x.shape
    tm = min(m, 512)
    return pl.pallas_call(
        _prenorm_kernel,
        grid=(m // tm,),
        in_specs=[pl.BlockSpec((tm, d), lambda i: (i, 0)), pl.BlockSpec((1, d), lambda i: (0, 0))],
        out_specs=pl.BlockSpec((tm, d), lambda i: (i, 0)),
        out_shape=jax.ShapeDtypeStruct((m, d), BF16),
        compiler_params=_cparams("arbitrary"),
        name="prenorm",
    )(x, g.reshape(1, d))


def _head_norm_rope(acc, gain, cos, sin, out_ref):
    for hh in range(acc.shape[1] // HEAD_DIM):
        sl = slice(hh * HEAD_DIM, (hh + 1) * HEAD_DIM)
        y = acc[:, sl]
        ms = jnp.mean(y * y, axis=-1, keepdims=True)
        y = y * lax.rsqrt(ms + EPS) * gain
        if cos is not None:
            y = y * cos + pltpu.roll(y, HEAD_DIM // 2, 1) * sin
        out_ref[:, sl] = y


def _in_proj_kernel(xp_ref, xs_ref, w_ref, gain_ref, cp_ref, sp_ref, cs_ref, ss_ref,
                    op_ref, os_ref, wbf_ref, *, normed_fn, tiles_per_part, rope):
    j = pl.program_id(0)
    m = pl.program_id(1)

    @pl.when(m == 0)
    def _():
        wbf_ref[...] = w_ref[...].astype(BF16)

    normed = normed_fn(j // tiles_per_part)
    plain = jnp.logical_not(normed)
    gain = gain_ref[...]

    def emit(x_ref, c_ref, s_ref, out_ref):
        acc = jnp.dot(x_ref[...], wbf_ref[...], preferred_element_type=F32)

        @pl.when(normed)
        def _():
            if rope:
                _head_norm_rope(acc, gain, c_ref[...], s_ref[...], out_ref)
            else:
                _head_norm_rope(acc, gain, None, None, out_ref)

        @pl.when(plain)
        def _():
            out_ref[...] = acc

    emit(xp_ref, cp_ref, sp_ref, op_ref)

    @pl.when(m == pl.num_programs(1) - 1)
    def _():
        emit(xs_ref, cs_ref, ss_ref, os_ref)


def _in_proj(hp, hs, w, layer, gains, rope_p, rope_s, *, n_out, normed_fn, rope):
    p_rows, k = hp.shape
    s_rows = hs.shape[0]
    tm = min(PROJ_TM, p_rows)
    tn = min(PROJ_TN, k)
    tiles_per_part = k // tn
    cos_p, sin_p = rope_p
    cos_s, sin_s = rope_s
    pos_tiles = cos_p.shape[0] // tm
    grid = (n_out // tn, p_rows // tm)
    kern = functools.partial(_in_proj_kernel, normed_fn=normed_fn, tiles_per_part=tiles_per_part, rope=rope)
    return pl.pallas_call(
        kern,
        grid=grid,
        in_specs=[
            pl.BlockSpec((tm, k), lambda j, m: (m, 0)),
            pl.BlockSpec((s_rows, k), lambda j, m: (0, 0)),
            pl.BlockSpec((None, k, tn), lambda j, m: (layer, 0, j), pipeline_mode=pl.Buffered(1)),
            pl.BlockSpec((None, 1, HEAD_DIM), lambda j, m: (j // tiles_per_part, 0, 0)),
            pl.BlockSpec((tm, HEAD_DIM), lambda j, m: (m % pos_tiles, 0)),
            pl.BlockSpec((tm, HEAD_DIM), lambda j, m: (m % pos_tiles, 0)),
            pl.BlockSpec((s_rows, HEAD_DIM), lambda j, m: (0, 0)),
            pl.BlockSpec((s_rows, HEAD_DIM), lambda j, m: (0, 0)),
        ],
        out_specs=[
            pl.BlockSpec((tm, tn), lambda j, m: (m, j)),
            pl.BlockSpec((s_rows, tn), lambda j, m: (0, j)),
        ],
        out_shape=[
            jax.ShapeDtypeStruct((p_rows, n_out), F32),
            jax.ShapeDtypeStruct((s_rows, n_out), F32),
        ],
        scratch_shapes=[pltpu.VMEM((k, tn), BF16)],
        compiler_params=_cparams("arbitrary", "arbitrary"),
        name="in_proj",
    )(hp, hs, w, gains, cos_p, sin_p, cos_s, sin_s)


def _out_proj_kernel(zp_ref, zs_ref, w_ref, rp_ref, rs_ref, op_ref, os_ref, wbf_ref):
    m = pl.program_id(1)

    @pl.when(m == 0)
    def _():
        wbf_ref[...] = w_ref[...].astype(BF16)

    op_ref[...] = rp_ref[...] + jnp.dot(zp_ref[...], wbf_ref[...], preferred_element_type=F32)

    @pl.when(m == pl.num_programs(1) - 1)
    def _():
        os_ref[...] = rs_ref[...] + jnp.dot(zs_ref[...].astype(BF16), wbf_ref[...],
                                            preferred_element_type=F32)


def _out_proj(zp, zs, w, layer, xp, xs):
    p_rows, k = zp.shape
    s_rows = zs.shape[0]
    n = w.shape[2]
    tm = min(PROJ_TM, p_rows)
    tn = min(PROJ_TN, n)
    return pl.pallas_call(
        _out_proj_kernel,
        grid=(n // tn, p_rows // tm),
        in_specs=[
            pl.BlockSpec((tm, k), lambda j, m: (m, 0)),
            pl.BlockSpec((s_rows, k), lambda j, m: (0, 0)),
            pl.BlockSpec((None, k, tn), lambda j, m: (layer, 0, j), pipeline_mode=pl.Buffered(1)),
            pl.BlockSpec((tm, tn), lambda j, m: (m, j)),
            pl.BlockSpec((s_rows, tn), lambda j, m: (0, j)),
        ],
        out_specs=[
            pl.BlockSpec((tm, tn), lambda j, m: (m, j)),
            pl.BlockSpec((s_rows, tn), lambda j, m: (0, j)),
        ],
        out_shape=[
            jax.ShapeDtypeStruct((p_rows, n), F32),
            jax.ShapeDtypeStruct((s_rows, n), F32),
        ],
        scratch_shapes=[pltpu.VMEM((k, tn), BF16)],
        compiler_params=_cparams("arbitrary", "arbitrary"),
        name="out_proj",
    )(zp, zs, w, xp, xs)


def _to_native_kernel(*refs, n_layers, n_arrays):
    srcs, dsts = refs[:n_layers * n_arrays], refs[n_layers * n_arrays:]
    for layer in range(n_layers):
        @pl.when(pl.program_id(0) == layer)
        def _():
            for a, dst in enumerate(dsts):
                src = srcs[layer * n_arrays + a]
                for hh in range(dst.shape[1]):
                    dst[:, hh, :] = src[:, hh * HEAD_DIM:(hh + 1) * HEAD_DIM]


def _to_native(parts_by_layer, part_ids, *, heads):
    n_layers = len(parts_by_layer)
    m = parts_by_layer[0].shape[0]
    hg = min(8, heads)
    n_hg = heads // hg
    tm = min(PROJ_TM, m)
    n_m = m // tm
    n = len(part_ids)

    def src_spec(layer, part):
        def index_map(l, i, c):
            row = jnp.where(l < layer, 0, jnp.where(l > layer, n_m - 1, i))
            col = jnp.where(l < layer, 0, jnp.where(l > layer, n_hg - 1, c))
            return row, part * n_hg + col
        return pl.BlockSpec((tm, hg * HEAD_DIM), index_map)

    return pl.pallas_call(
        functools.partial(_to_native_kernel, n_layers=n_layers, n_arrays=n),
        grid=(n_layers, n_m, n_hg),
        in_specs=[src_spec(layer, p) for layer in range(n_layers) for p in part_ids],
        out_specs=[pl.BlockSpec((None, tm, hg, HEAD_DIM), lambda l, i, c: (l, i, c, 0))] * n,
        out_shape=[jax.ShapeDtypeStruct((n_layers, m, heads, HEAD_DIM), F32)] * n,
        compiler_params=_cparams("arbitrary", "arbitrary", "arbitrary"),
        name="to_native",
    )(*[parts for parts in parts_by_layer for _ in part_ids])


def _silu(x):
    return x * (1.0 / (1.0 + jnp.exp(-x)))


def _qk(q, k):
    return lax.dot_general(q.astype(BF16), k.astype(BF16), (((1,), (1,)), ((), ())),
                           preferred_element_type=F32)


def _pv(p, v):
    return jnp.dot(p.astype(BF16), v.astype(BF16), preferred_element_type=F32)


def _softmax_pv(s, v):
    mx = jnp.max(s, axis=-1, keepdims=True)
    p = jnp.exp(s - mx)
    den = jnp.sum(p, axis=-1, keepdims=True)
    o = _pv(p, v) * (1.0 / den)
    return o, mx + jnp.log(den)


def _merge_groups(os_, lses):
    mx = functools.reduce(jnp.maximum, lses)
    ws = [jnp.exp(l - mx) for l in lses]
    tot = functools.reduce(lambda a, b: a + b, ws)
    acc = functools.reduce(lambda a, b: a + b, [w * o for w, o in zip(ws, os_)])
    return acc * (1.0 / tot)


def _attn_a_prompt_kernel(*refs, seq):
    qkv = refs[:3 * N_GROUPS]
    gate_ref = refs[3 * N_GROUPS]
    z_ref = refs[3 * N_GROUPS + 1]
    o_scr, lse_scr = refs[3 * N_GROUPS + 2:]
    scale = HEAD_DIM ** -0.5

    for g, (win, dil) in enumerate(A_GROUPS):
        q_ref, k_ref, v_ref = qkv[3 * g:3 * g + 3]
        steps = win // dil
        n_blocks = (seq // dil) // steps
        r = lax.broadcasted_iota(jnp.int32, (steps, steps), 0)
        c = lax.broadcasted_iota(jnp.int32, (steps, steps), 1)
        first_mask = c <= r
        r2 = lax.broadcasted_iota(jnp.int32, (steps, 2 * steps), 0)
        c2 = lax.broadcasted_iota(jnp.int32, (steps, 2 * steps), 1)
        rest_mask = (c2 >= r2) & (c2 <= r2 + steps)

        def rows(start, n):
            return pl.ds(start, n, stride=dil) if dil > 1 else pl.ds(start, n)

        for phase in range(dil):
            for blk in range(n_blocks):
                q_rows = rows(phase + dil * steps * blk, steps)
                if blk == 0:
                    k_rows, mask = q_rows, first_mask
                else:
                    k_rows, mask = rows(phase + dil * steps * (blk - 1), 2 * steps), rest_mask
                s = _qk(q_ref[q_rows, :], k_ref[k_rows, :]) * scale
                s = jnp.where(mask, s, NEG_INF)
                o, lse = _softmax_pv(s, v_ref[k_rows, :])
                o_scr[g, q_rows, :] = o
                lse_scr[g, q_rows, :] = jnp.broadcast_to(lse, (steps, HEAD_DIM))

    merged = _merge_groups([o_scr[g] for g in range(N_GROUPS)], [lse_scr[g] for g in range(N_GROUPS)])
    z_ref[...] = (merged * _silu(gate_ref[...])).astype(z_ref.dtype)


def _attn_a_prompt(parts, *, batch, seq, heads):
    def col(part):
        return pl.BlockSpec((seq, HEAD_DIM), lambda b, h: (b, part * heads + h))

    n_in = 3 * N_GROUPS + 1
    return pl.pallas_call(
        functools.partial(_attn_a_prompt_kernel, seq=seq),
        grid=(batch, heads),
        in_specs=[col(p) for p in range(n_in)],
        out_specs=pl.BlockSpec((seq, HEAD_DIM), lambda b, h: (b, h)),
        out_shape=jax.ShapeDtypeStruct((batch * seq, heads * HEAD_DIM), BF16),
        scratch_shapes=[pltpu.VMEM((N_GROUPS, seq, HEAD_DIM), F32),
                        pltpu.VMEM((N_GROUPS, seq, HEAD_DIM), F32)],
        compiler_params=_cparams("arbitrary", "arbitrary"),
        name="attn_a_prompt",
    )(*([parts] * n_in))


def _attn_a_sample_kernel(*refs, n_new):
    new = refs[:3 * N_GROUPS]
    gate_ref = refs[3 * N_GROUPS]
    caches = refs[3 * N_GROUPS + 1:5 * N_GROUPS + 1]
    z_ref = refs[5 * N_GROUPS + 1]
    scale = HEAD_DIM ** -0.5
    ht = gate_ref.shape[1] // HEAD_DIM
    rows_t = ht * n_new
    pad = jnp.zeros((HEAD_DIM - rows_t, HEAD_DIM), BF16)

    def tile_rows(ref):
        return jnp.concatenate([ref[:, hs * HEAD_DIM:(hs + 1) * HEAD_DIM] for hs in range(ht)], axis=0)

    outs, lses = [], []
    for g, (win, dil) in enumerate(A_GROUPS):
        q_ref, kn_ref, vn_ref = new[3 * g:3 * g + 3]
        kc_ref, vc_ref = caches[2 * g:2 * g + 2]
        buf = kc_ref.shape[0]
        n_cached = buf * ht
        k_all = jnp.concatenate([kc_ref[...].reshape(n_cached, HEAD_DIM).astype(BF16),
                                 tile_rows(kn_ref).astype(BF16), pad], axis=0)
        v_all = jnp.concatenate([vc_ref[...].reshape(n_cached, HEAD_DIM).astype(BF16),
                                 tile_rows(vn_ref).astype(BF16), pad], axis=0)
        s = _qk(tile_rows(q_ref), k_all) * scale
        r = lax.broadcasted_iota(jnp.int32, s.shape, 0)
        c = lax.broadcasted_iota(jnp.int32, s.shape, 1)
        cached = c < n_cached
        c_new = c - n_cached
        key_head = jnp.where(cached, lax.rem(c, ht), lax.div(c_new, n_new))
        key_pos = jnp.where(cached, lax.div(c, ht), buf + lax.rem(c_new, n_new))
        back = buf + lax.rem(r, n_new) - key_pos
        valid = ((key_head == lax.div(r, n_new)) & (back >= 0) & (back <= win)
                 & ((back & (dil - 1)) == 0))
        s = jnp.where(valid, s, NEG_INF)
        o, lse = _softmax_pv(s, v_all)
        outs.append(o)
        lses.append(lse)
    merged = _merge_groups(outs, lses)
    for hs in range(ht):
        cols = slice(hs * HEAD_DIM, (hs + 1) * HEAD_DIM)
        z_ref[:, cols] = merged[hs * n_new:(hs + 1) * n_new, :] * _silu(gate_ref[:, cols])


def _attn_a_sample(parts, caches, layer, *, batch, n_new, heads):
    hg = min(8, heads)
    n_hg = heads // hg

    def col(part):
        return pl.BlockSpec((n_new, hg * HEAD_DIM), lambda b, h: (b, part * n_hg + h))

    def cache_spec(c):
        return pl.BlockSpec((None, None, c.shape[2], hg, HEAD_DIM), lambda b, h: (layer, b, 0, h, 0))

    n_in = 3 * N_GROUPS + 1
    flat = [c for kv in caches for c in kv]
    return pl.pallas_call(
        functools.partial(_attn_a_sample_kernel, n_new=n_new),
        grid=(batch, n_hg),
        in_specs=[col(p) for p in range(n_in)] + [cache_spec(c) for c in flat],
        out_specs=pl.BlockSpec((n_new, hg * HEAD_DIM), lambda b, h: (b, h)),
        out_shape=jax.ShapeDtypeStruct((batch * n_new, heads * HEAD_DIM), F32),
        compiler_params=_cparams("arbitrary", "arbitrary"),
        name="attn_a_sample",
    )(*([parts] * n_in), *flat)


def _log_sigmoid(x):
    return jnp.minimum(x, 0.0) - jnp.log(1.0 + jnp.exp(-jnp.abs(x)))


def _cumsum_lanes(x):
    n = x.shape[-1]
    lane = lax.broadcasted_iota(jnp.int32, x.shape, x.ndim - 1)
    shift = 1
    while shift < n:
        x = x + jnp.where(lane >= shift, pltpu.roll(x, shift, x.ndim - 1), 0.0)
        shift *= 2
    return x


def _forget_kernel(h_ref, w_ref, wt_ref, b_ref, bt_ref, lf_ref, lft_ref, ct_ref):
    h = h_ref[...]
    lf_ref[...] = _log_sigmoid(jnp.dot(h, w_ref[...].astype(BF16), preferred_element_type=F32) + b_ref[...])
    lft = _log_sigmoid(_qk(wt_ref[...], h) + bt_ref[...])
    lft_ref[...] = lft
    ct_ref[...] = _cumsum_lanes(lft)


def _forget_rows_kernel(h_ref, w_ref, b_ref, lf_ref):
    lf_ref[...] = _log_sigmoid(
        jnp.dot(h_ref[...], w_ref[...].astype(BF16), preferred_element_type=F32) + b_ref[...])


def _forget_rows(h, w_f, b_f):
    m, d = h.shape
    nh = w_f.shape[1]
    return pl.pallas_call(
        _forget_rows_kernel,
        grid=(1,),
        in_specs=[pl.BlockSpec((m, d), lambda i: (0, 0)),
                  pl.BlockSpec((d, nh), lambda i: (0, 0)),
                  pl.BlockSpec((1, nh), lambda i: (0, 0))],
        out_specs=pl.BlockSpec((m, nh), lambda i: (0, 0)),
        out_shape=jax.ShapeDtypeStruct((m, nh), F32),
        compiler_params=_cparams("arbitrary"),
        name="forget_rows",
    )(h, w_f, b_f.reshape(1, nh))


def _forget(h, w_f, b_f, *, batch):
    m, d = h.shape
    rows = m // batch
    nh = w_f.shape[1]
    t_shape = jax.ShapeDtypeStruct((batch, nh, rows), F32)
    t_spec = pl.BlockSpec((None, nh, rows), lambda b: (b, 0, 0))
    return pl.pallas_call(
        _forget_kernel,
        grid=(batch,),
        in_specs=[
            pl.BlockSpec((rows, d), lambda b: (b, 0)),
            pl.BlockSpec((d, nh), lambda b: (0, 0)),
            pl.BlockSpec((nh, d), lambda b: (0, 0)),
            pl.BlockSpec((1, nh), lambda b: (0, 0)),
            pl.BlockSpec((nh, 1), lambda b: (0, 0)),
        ],
        out_specs=[pl.BlockSpec((rows, nh), lambda b: (b, 0)), t_spec, t_spec],
        out_shape=[jax.ShapeDtypeStruct((m, nh), F32), t_shape, t_shape],
        compiler_params=_cparams("arbitrary"),
        name="forget",
    )(h, w_f, w_f.T, b_f.reshape(1, nh), b_f.reshape(nh, 1))


def _attn_f_prompt_kernel(q_ref, k_ref, v_ref, gate_ref, c_ref, z_ref, *, seq, tq):
    scale = HEAD_DIM ** -0.5
    hh = pl.program_id(1) % c_ref.shape[0]
    c_row = c_ref[pl.ds(hh, 1), :]
    r = lax.broadcasted_iota(jnp.int32, (tq, tq), 0)
    c = lax.broadcasted_iota(jnp.int32, (tq, tq), 1)
    diag_mask = c <= r

    for qi in range(seq // tq):
        q_rows = slice(qi * tq, (qi + 1) * tq)
        q = q_ref[q_rows, :].astype(BF16)
        c_q = jnp.broadcast_to(c_row[:, q_rows], (tq, tq)).T
        m_i = jnp.full((tq, 1), NEG_INF, F32)
        l_i = jnp.zeros((tq, 1), F32)
        acc = jnp.zeros((tq, HEAD_DIM), F32)
        for kj in range(qi + 1):
            k_rows = slice(kj * tq, (kj + 1) * tq)
            s = _qk(q, k_ref[k_rows, :]) * scale + (c_q - c_row[:, k_rows])
            if kj == qi:
                s = jnp.where(diag_mask, s, NEG_INF)
            m_new = jnp.maximum(m_i, jnp.max(s, axis=-1, keepdims=True))
            alpha = jnp.exp(m_i - m_new)
            p = jnp.exp(s - m_new)
            l_i = alpha * l_i + jnp.sum(p, axis=-1, keepdims=True)
            acc = alpha * acc + _pv(p, v_ref[k_rows, :])
            m_i = m_new
        o = acc * (1.0 / l_i)
        z_ref[q_rows, :] = (o * _silu(gate_ref[q_rows, :])).astype(z_ref.dtype)


def _attn_f_prompt(parts, c_t, *, batch, seq, heads):
    def col(part):
        return pl.BlockSpec((seq, HEAD_DIM), lambda b, h: (b, part * heads + h))

    tq = min(FOX_TQ, seq)
    c_rows = min(8, heads)
    return pl.pallas_call(
        functools.partial(_attn_f_prompt_kernel, seq=seq, tq=tq),
        grid=(batch, heads),
        in_specs=[col(0), col(1), col(2), col(3),
                  pl.BlockSpec((None, c_rows, seq), lambda b, h: (b, h // c_rows, 0))],
        out_specs=pl.BlockSpec((seq, HEAD_DIM), lambda b, h: (b, h)),
        out_shape=jax.ShapeDtypeStruct((batch * seq, heads * HEAD_DIM), BF16),
        compiler_params=_cparams("arbitrary", "arbitrary"),
        name="attn_f_prompt",
    )(parts, parts, parts, parts, c_t)


def _page_prefix_kernel(*refs, pages_per_step):
    g = pages_per_step
    lft_refs = refs[1:1 + g]
    c_ref, tot_ref, carry_ref = refs[1 + g:]
    j = pl.program_id(1)
    page = carry_ref.shape[1]

    @pl.when(j == 0)
    def _():
        carry_ref[...] = jnp.zeros_like(carry_ref)

    carry = carry_ref[...]
    for i in range(g):
        c = _cumsum_lanes(lft_refs[i][...]) + carry
        c_ref[:, i * page:(i + 1) * page] = c
        carry = jnp.broadcast_to(c[:, -1:], c.shape)
    carry_ref[...] = carry
    tot_ref[...] = carry


def _page_prefix(lft_pages, page_table, layer):
    _, _, nh, page = lft_pages.shape
    batch, n_pages = page_table.shape
    g = PREFIX_PAGES_PER_STEP if n_pages % PREFIX_PAGES_PER_STEP == 0 else 1

    def page_spec(i):
        return pl.BlockSpec((None, None, nh, page), lambda b, j, pt: (layer, pt[b, j * g + i], 0, 0))

    return pl.pallas_call(
        functools.partial(_page_prefix_kernel, pages_per_step=g),
        grid_spec=pltpu.PrefetchScalarGridSpec(
            num_scalar_prefetch=1,
            grid=(batch, n_pages // g),
            in_specs=[page_spec(i) for i in range(g)],
            out_specs=[pl.BlockSpec((None, nh, g * page), lambda b, j, pt: (b, 0, j)),
                       pl.BlockSpec((None, nh, page), lambda b, j, pt: (b, 0, 0))],
            scratch_shapes=[pltpu.VMEM((nh, page), F32)],
        ),
        out_shape=[jax.ShapeDtypeStruct((batch, nh, n_pages * page), F32),
                   jax.ShapeDtypeStruct((batch, nh, page), F32)],
        compiler_params=_cparams("arbitrary", "arbitrary"),
        name="page_prefix",
    )(page_table, *([lft_pages] * g))


def _attn_f_sample_kernel(*refs, heads, n_new, page, pages_per_step):
    g = pages_per_step
    pt_ref = refs[0]
    del pt_ref
    q_ref, kn_ref, vn_ref, gate_ref, lfn_ref, c_ref, tot_ref = refs[1:8]
    k_refs = refs[8:8 + g]
    v_refs = refs[8 + g:8 + 2 * g]
    z_ref = refs[8 + 2 * g]
    q_scr, m_scr, l_scr, acc_scr = refs[9 + 2 * g:]
    j = pl.program_id(1)
    scale = HEAD_DIM ** -0.5
    n_tiles, rows_t, _ = q_scr.shape
    ht = rows_t // n_new

    def head_cols(h):
        return slice(h * HEAD_DIM, (h + 1) * HEAD_DIM)

    def tile_rows(ref, tile):
        return jnp.concatenate([ref[:, head_cols(tile * ht + hs)] for hs in range(ht)], axis=0)

    @pl.when(j == 0)
    def _():
        m_scr[...] = jnp.full_like(m_scr, NEG_INF)
        l_scr[...] = jnp.zeros_like(l_scr)
        acc_scr[...] = jnp.zeros_like(acc_scr)
        for tile in range(n_tiles):
            q_scr[tile] = tile_rows(q_ref, tile).astype(BF16)

    lfn = lfn_ref[...]
    row = lax.broadcasted_iota(jnp.int32, lfn.shape, 0)
    shift = 1
    while shift < n_new:
        lfn = lfn + jnp.where(row >= shift, pltpu.roll(lfn, shift, 0), 0.0)
        shift *= 2
    tot = tot_ref[...]

    def tile_col(tile, with_total):
        cols = []
        for h in range(tile * ht, (tile + 1) * ht):
            cols.append(lfn[:, h:h + 1] + tot[h:h + 1, 0:1] if with_total else lfn[:, h:h + 1])
        return jnp.concatenate(cols, axis=0)

    def attend(tile, kb, vb, bias, mask):
        s = lax.dot_general(q_scr[tile], kb, (((1,), (1,)), ((), ())), preferred_element_type=F32)
        s = jnp.where(mask, s * scale + bias, NEG_INF)
        m_old = m_scr[tile]
        m_new = jnp.maximum(m_old, jnp.max(s, axis=-1, keepdims=True))
        alpha = jnp.exp(m_old - m_new)
        p = jnp.exp(s - m_new)
        l_scr[tile] = alpha * l_scr[tile] + jnp.sum(p, axis=-1, keepdims=True)
        acc_scr[tile] = alpha * acc_scr[tile] + jnp.dot(p.astype(BF16), vb, preferred_element_type=F32)
        m_scr[tile] = m_new

    def tile_keys(page_refs, tile):
        blocks = [r[:, tile * ht:(tile + 1) * ht, :].reshape(page * ht, HEAD_DIM) for r in page_refs]
        return (jnp.concatenate(blocks, axis=0) if len(blocks) > 1 else blocks[0]).astype(BF16)

    n_keys = g * page * ht
    key_head = lax.rem(lax.broadcasted_iota(jnp.int32, (rows_t, n_keys), 1), ht)
    own_head = key_head == lax.div(lax.broadcasted_iota(jnp.int32, (rows_t, n_keys), 0), n_new)
    for tile in range(n_tiles):
        attend(tile, tile_keys(k_refs, tile), tile_keys(v_refs, tile),
               tile_col(tile, True) - c_ref[tile], own_head)

    @pl.when(j == pl.num_programs(1) - 1)
    def _():
        pad = jnp.zeros((HEAD_DIM - rows_t, HEAD_DIM), F32)
        r = lax.broadcasted_iota(jnp.int32, (rows_t, HEAD_DIM), 0)
        c = lax.broadcasted_iota(jnp.int32, (rows_t, HEAD_DIM), 1)
        mask = (lax.div(c, n_new) == lax.div(r, n_new)) & (lax.rem(c, n_new) <= lax.rem(r, n_new))
        n_idx = lax.broadcasted_iota(jnp.int32, (n_new, HEAD_DIM), 0)
        c_idx = lax.broadcasted_iota(jnp.int32, (n_new, HEAD_DIM), 1)
        for tile in range(n_tiles):
            kn = jnp.concatenate([tile_rows(kn_ref, tile), pad], axis=0).astype(BF16)
            vn = jnp.concatenate([tile_rows(vn_ref, tile), pad], axis=0).astype(BF16)
            per_token = jnp.zeros((n_new, HEAD_DIM), F32)
            for hs in range(ht):
                h = tile * ht + hs
                per_token = jnp.where(lax.div(c_idx, n_new) == hs, lfn[:, h:h + 1], per_token)
            c_s = jnp.sum(jnp.where(lax.rem(c_idx, n_new) == n_idx, per_token, 0.0), axis=0, keepdims=True)
            attend(tile, kn, vn, tile_col(tile, False) - c_s, mask)
            o = acc_scr[tile] * (1.0 / l_scr[tile])
            for hs in range(ht):
                cols = head_cols(tile * ht + hs)
                z_ref[:, cols] = o[hs * n_new:(hs + 1) * n_new, :] * _silu(gate_ref[:, cols])


def _attn_f_sample(parts, lf_new, c_past, tot, k_pages, v_pages, page_table, layer, *, batch, n_new, heads):
    page = k_pages.shape[2]
    n_pages = page_table.shape[1]
    g = PAGES_PER_STEP if n_pages % PAGES_PER_STEP == 0 else 1
    width = heads * HEAD_DIM
    ht = min(8, heads)
    n_tiles = heads // ht
    rows_t = ht * n_new
    c_tiles = c_past.reshape(batch, n_tiles, ht, n_pages * page)
    c_tiles = jnp.swapaxes(c_tiles, 2, 3).reshape(batch, n_tiles, 1, n_pages * page * ht)

    def col(part):
        return pl.BlockSpec((n_new, width), lambda b, j, pt: (b, part))

    def page_spec(i):
        return pl.BlockSpec((None, None, page, heads, HEAD_DIM),
                            lambda b, j, pt: (layer, pt[b, j * g + i], 0, 0, 0))

    kern = functools.partial(_attn_f_sample_kernel, heads=heads, n_new=n_new, page=page, pages_per_step=g)
    return pl.pallas_call(
        kern,
        grid_spec=pltpu.PrefetchScalarGridSpec(
            num_scalar_prefetch=1,
            grid=(batch, n_pages // g),
            in_specs=[col(0), col(1), col(2), col(3),
                      pl.BlockSpec((n_new, heads), lambda b, j, pt: (b, 0)),
                      pl.BlockSpec((None, n_tiles, 1, g * page * ht), lambda b, j, pt: (b, 0, 0, j)),
                      pl.BlockSpec((None, heads, page), lambda b, j, pt: (b, 0, 0))]
            + [page_spec(i) for i in range(g)] + [page_spec(i) for i in range(g)],
            out_specs=pl.BlockSpec((n_new, width), lambda b, j, pt: (b, 0)),
            scratch_shapes=[pltpu.VMEM((n_tiles, rows_t, HEAD_DIM), BF16),
                            pltpu.VMEM((n_tiles, rows_t, 1), F32),
                            pltpu.VMEM((n_tiles, rows_t, 1), F32),
                            pltpu.VMEM((n_tiles, rows_t, HEAD_DIM), F32)],
        ),
        out_shape=jax.ShapeDtypeStruct((batch * n_new, width), F32),
        compiler_params=_cparams("arbitrary", "arbitrary"),
        name="attn_f_sample",
    )(page_table, parts, parts, parts, parts, lf_new, c_tiles, tot,
      *([k_pages] * g), *([v_pages] * g))


def _rope_tables(pos):
    half = HEAD_DIM // 2
    freqs = ROPE_THETA ** (-jnp.arange(half, dtype=F32) * 2.0 / HEAD_DIM)
    ang = pos.astype(F32)[:, None] * freqs[None, :]
    cos, sin = jnp.cos(ang), jnp.sin(ang)
    return jnp.concatenate([cos, cos], axis=-1), jnp.concatenate([-sin, sin], axis=-1)


def _a_normed(part):
    return (part % 3 != 2) & (part != 3 * N_GROUPS)


def _f_normed(part):
    return part < 2


def kernel(x_prompt, x_sample, cache_win0_k, cache_win0_v, cache_win1_k, cache_win1_v, cache_win2_k, cache_win2_v,
           cache_fox_k, cache_fox_v, cache_fox_logf, page_table,
           a_norm, a_w_in, a_q_gain, a_k_gain, a_w_out,
           f_norm, f_w_in, f_b_forget, f_q_gain, f_k_gain, f_w_out):
    batch, seq, d = x_prompt.shape
    dec_batch, n_new, _ = x_sample.shape
    heads = d // HEAD_DIM
    depth = a_norm.shape[0] + f_norm.shape[0]
    past = page_table.shape[1] * cache_fox_k.shape[2]

    win_caches = [(cache_win0_k, cache_win0_v), (cache_win1_k, cache_win1_v), (cache_win2_k, cache_win2_v)]
    fox_lft = jnp.swapaxes(cache_fox_logf, 2, 3)

    rope_p = _rope_tables(jnp.arange(seq))
    rope_s = _rope_tables(past + jnp.tile(jnp.arange(n_new), dec_batch))
    ones = jnp.ones((HEAD_DIM,), F32)

    yp = x_prompt.reshape(batch * seq, d)
    ys = x_sample.reshape(dec_batch * n_new, d)
    pwk = [[] for _ in range(N_GROUPS)]
    pwv = [[] for _ in range(N_GROUPS)]
    sw_new = [[] for _ in range(2 * N_GROUPS)]
    pfl, sfk, sfv, sfl = [], [], [], []
    n_a, n_f = a_norm.shape[0], f_norm.shape[0]
    a_parts, f_parts = [], []
    full_ids = [p for g, (win, _) in enumerate(A_GROUPS) if win >= seq for p in (3 * g + 1, 3 * g + 2)]

    def part_p(parts, p):
        return parts[:, p * d:(p + 1) * d].reshape(batch, seq, heads, HEAD_DIM)

    def part_s(parts, p):
        return parts[:, p * d:(p + 1) * d].reshape(dec_batch, n_new, heads, HEAD_DIM)

    for layer in range(depth):
        li = layer // 2
        if layer % 2 == 0:
            hp = _prenorm(yp, a_norm[li])
            hs = _prenorm(ys, a_norm[li])
            rows = []
            for g in range(N_GROUPS):
                rows += [a_q_gain[li, g], a_k_gain[li, g], ones]
            gains = jnp.stack(rows + [ones]).reshape(3 * N_GROUPS + 1, 1, HEAD_DIM)
            parts_p, parts_s = _in_proj(hp, hs, a_w_in, li, gains, rope_p, rope_s,
                                        n_out=a_w_in.shape[2], normed_fn=_a_normed, rope=True)
            zp = _attn_a_prompt(parts_p, batch=batch, seq=seq, heads=heads)
            zs = _attn_a_sample(parts_s, win_caches, li, batch=dec_batch, n_new=n_new, heads=heads)
            yp, ys = _out_proj(zp, zs, a_w_out, li, yp, ys)
            a_parts.append(parts_p)
            for g, (win, _) in enumerate(A_GROUPS):
                keep = min(win, seq)
                if keep < seq:
                    pwk[g].append(part_p(parts_p, 3 * g + 1)[:, seq - keep:])
                    pwv[g].append(part_p(parts_p, 3 * g + 2)[:, seq - keep:])
                for kv in range(2):
                    sw_new[2 * g + kv].append(part_s(parts_s, 3 * g + 1 + kv))
        else:
            hp = _prenorm(yp, f_norm[li])
            hs = _prenorm(ys, f_norm[li])
            gains = jnp.stack([f_q_gain[li], f_k_gain[li], ones, ones]).reshape(4, 1, HEAD_DIM)
            parts_p, parts_s = _in_proj(hp, hs, f_w_in, li, gains, rope_p, rope_s,
                                        n_out=4 * d, normed_fn=_f_normed, rope=False)
            w_f = f_w_in[li, :, 4 * d:]
            lf_p, _, c_t = _forget(hp, w_f, f_b_forget[li], batch=batch)
            lf_s = _forget_rows(hs, w_f, f_b_forget[li])
            zp = _attn_f_prompt(parts_p, c_t, batch=batch, seq=seq, heads=heads)
            c_past, tot = _page_prefix(fox_lft, page_table, li)
            zs = _attn_f_sample(parts_s, lf_s, c_past, tot, cache_fox_k, cache_fox_v, page_table, li,
                                batch=dec_batch, n_new=n_new, heads=heads)
            yp, ys = _out_proj(zp, zs, f_w_out, li, yp, ys)
            f_parts.append(parts_p)
            pfl.append(lf_p.reshape(batch, seq, heads))
            sfk.append(part_s(parts_s, 1))
            sfv.append(part_s(parts_s, 2))
            sfl.append(lf_s.reshape(dec_batch, n_new, heads))

    def native(buf, n_layers):
        return buf.reshape(n_layers, batch, seq, heads, HEAD_DIM)

    outs = [yp.reshape(batch, seq, d), ys.reshape(dec_batch, n_new, d)]
    full = iter(_to_native(a_parts, full_ids, heads=heads) if full_ids else [])
    for g, (win, _) in enumerate(A_GROUPS):
        if win >= seq:
            outs += [native(next(full), n_a), native(next(full), n_a)]
        else:
            outs += [jnp.stack(pwk[g]), jnp.stack(pwv[g])]
    for cache, rows in zip([c for kv in win_caches for c in kv], sw_new):
        outs.append(jnp.concatenate([cache[:, :, n_new:], jnp.stack(rows)], axis=2))
    pf_full = _to_native(f_parts, [1, 2], heads=heads)
    outs += [native(pf_full[0], n_f), native(pf_full[1], n_f),
             jnp.stack(pfl), jnp.stack(sfk), jnp.stack(sfv), jnp.stack(sfl)]
    return tuple(outs)
```
